```python
import math
import jax
import jax.numpy as jnp
from jax import lax
import numpy as np

D_MODEL = 1024
BATCH = 8
SEQ = 2048
DEPTH = 4
DEC_BATCH = 128
DEC_SEQ = 1
PAST_LEN = 16384
PAGE_SIZE = 128

M_WIDTH = D_MODEL
M_HEADS = 4
M_HD = M_WIDTH // M_HEADS
CONV_W = 4
M_CHUNK = 128
R_WIDTH = D_MODEL
R_HD = 64
R_HEADS = R_WIDTH // R_HD
R_LORA_W = 64
R_LORA_A = 64
R_SHIFT_W = 3 * R_WIDTH + R_LORA_W + R_LORA_A
S_WIDTH = D_MODEL
S_GROUP = 16
S_GROUPS = S_WIDTH // S_GROUP
S_STATE = 64
N_BRANCH = 3
IN_SPLITS = (M_WIDTH, M_HEADS, M_HEADS, M_WIDTH, M_WIDTH, R_SHIFT_W, R_WIDTH, S_WIDTH, S_WIDTH, N_BRANCH * D_MODEL)
N_IN = sum(IN_SPLITS)
ALPHA = (2.0 * DEPTH) ** 0.25
BETA = (8.0 * DEPTH) ** -0.25
LN_EPS = 1e-5
RWKV_GN_EPS = 64e-5
NEG = -1e30

kernel_name = 'hybrid_mlstm_rwkv7_s5_step'


def _f32(a):
    return a.astype(jnp.float32)


def _layer_norm(x, g, b, eps=LN_EPS):
    xf = _f32(x)
    mu = xf.mean(-1, keepdims=True)
    var = jnp.square(xf - mu).mean(-1, keepdims=True)
    y = (xf - mu) * lax.rsqrt(var + eps) * _f32(g) + _f32(b)
    return y.astype(x.dtype)


def _head_norm(h, eps):
    mu = h.mean(-1, keepdims=True)
    var = jnp.square(h - mu).mean(-1, keepdims=True)
    return (h - mu) * lax.rsqrt(var + eps)


def _split_cols(z):
    offsets = [int(o) for o in np.cumsum(IN_SPLITS)[:-1]]
    return jnp.split(z, offsets, axis=-1)


def _causal_conv(xm, buf, w, b):
    t = xm.shape[1]
    xp = jnp.concatenate([buf.astype(xm.dtype), xm], axis=1)
    out = b
    for j in range(CONV_W):
        out = out + w[j] * xp[:, j:j + t]
    return out, xp[:, -(CONV_W - 1):]


def _mlstm_chunkwise(q, k, v, ig, fg, c0, n0, m0):
    bsz, t, h, d = q.shape
    L = min(M_CHUNK, t)
    nc = -(-t // L)
    pad = nc * L - t
    logf = jax.nn.log_sigmoid(fg)
    if pad:
        pw = ((0, 0), (0, pad), (0, 0), (0, 0))
        q = jnp.pad(q, pw)
        k = jnp.pad(k, pw)
        v = jnp.pad(v, pw)
        logf = jnp.pad(logf, pw[:3])
        ig = jnp.pad(ig, pw[:3], constant_values=NEG)

    def to_chunks(a):
        a = a.reshape((bsz, nc, L) + a.shape[2:])
        return jnp.transpose(a, (1, 0, 3, 2) + tuple(range(4, a.ndim)))

    causal = jnp.tril(jnp.ones((L, L), dtype=bool))

    def body(carry, xs):
        c, n, m = carry
        qc, kc, vc, igc, lfc = xs
        b = jnp.cumsum(lfc, axis=-1)
        log_intra = jnp.where(causal, b[..., :, None] - b[..., None, :] + igc[..., None, :], NEG)
        log_init = b + m[..., None]
        m_t = jnp.maximum(log_init, log_intra.max(-1))
        w_intra = jnp.exp(log_intra - m_t[..., None])
        w_init = jnp.exp(log_init - m_t)
        s = jnp.einsum('bhtd,bhsd->bhts', qc, kc) * w_intra
        num = jnp.einsum('bhts,bhsv->bhtv', s, vc) + w_init[..., None] * jnp.einsum('bhtk,bhkv->bhtv', qc, c)
        den = s.sum(-1) + w_init * jnp.einsum('bhtk,bhk->bht', qc, n)
        hc = num / jnp.maximum(jnp.abs(den), jnp.exp(-m_t))[..., None]
        b_last = b[..., -1]
        log_state = b_last[..., None] - b + igc
        m_new = jnp.maximum(b_last + m, log_state.max(-1))
        w_state = jnp.exp(log_state - m_new[..., None])
        decay = jnp.exp(b_last + m - m_new)
        c_new = decay[..., None, None] * c + jnp.einsum('bhs,bhsk,bhsv->bhkv', w_state, kc, vc)
        n_new = decay[..., None] * n + jnp.einsum('bhs,bhsk->bhk', w_state, kc)
        return (c_new, n_new, m_new), hc

    xs = (to_chunks(q), to_chunks(k), to_chunks(v), to_chunks(ig), to_chunks(logf))
    (c_f, n_f, m_f), hs = lax.scan(body, (c0, n0, m0), xs)
    hs = jnp.transpose(hs, (1, 0, 3, 2, 4)).reshape(bsz, nc * L, h, d)[:, :t]
    return hs, c_f, n_f, m_f


def _rwkv7_recurrence(r, w, k, v, kk, a, s0):
    def step(s, xs):
        rt, wt, kt, vt, kkt, at = xs
        sa = jnp.einsum('bhij,bhj->bhi', s, -kkt)
        s = s * wt[:, :, None, :] + sa[..., None] * (kkt * at)[:, :, None, :] + vt[..., None] * kt[:, :, None, :]
        return s, jnp.einsum('bhij,bhj->bhi', s, rt)
    xs = tuple(jnp.moveaxis(z, 1, 0) for z in (r, w, k, v, kk, a))
    s_f, y = lax.scan(step, s0, xs)
    return jnp.moveaxis(y, 0, 1), s_f


def _s5_scan(u, s0_re, s0_im, lam_re, lam_im, log_dt, b_re, b_im):
    lam_re = jnp.minimum(lam_re, -1e-4)
    dt = jnp.exp(log_dt)[:, None]
    mag = jnp.exp(lam_re * dt)
    lb_re = mag * jnp.cos(lam_im * dt)
    lb_im = mag * jnp.sin(lam_im * dt)
    den = lam_re * lam_re + lam_im * lam_im
    nr = lb_re - 1.0
    coef_re = (nr * lam_re + lb_im * lam_im) / den
    coef_im = (lb_im * lam_re - nr * lam_im) / den
    bb_re = coef_re[..., None] * b_re - coef_im[..., None] * b_im
    bb_im = coef_re[..., None] * b_im + coef_im[..., None] * b_re
    bu_re = jnp.einsum('gpc,btgc->btgp', bb_re, u)
    bu_im = jnp.einsum('gpc,btgc->btgp', bb_im, u)
    bu_re = bu_re.at[:, 0].add(lb_re * s0_re - lb_im * s0_im)
    bu_im = bu_im.at[:, 0].add(lb_re * s0_im + lb_im * s0_re)
    a_re = jnp.broadcast_to(lb_re, bu_re.shape)
    a_im = jnp.broadcast_to(lb_im, bu_im.shape)

    def combine(e1, e2):
        a1r, a1i, b1r, b1i = e1
        a2r, a2i, b2r, b2i = e2
        return (a2r * a1r - a2i * a1i, a2r * a1i + a2i * a1r,
                a2r * b1r - a2i * b1i + b2r, a2r * b1i + a2i * b1r + b2i)

    _, _, s_re, s_im = lax.associative_scan(combine, (a_re, a_im, bu_re, bu_im), axis=1)
    return s_re, s_im


def _layer(x, st, p):
    c0, n0, m0, conv0, wkv0, shift0, sre0, sim0 = st
    bsz, t, _ = x.shape
    dt_ = x.dtype
    xm, ig, fg, og, zm, rcols, zr, u, zs, gm = _split_cols(x @ p['w_in'])

    xc, conv_new = _causal_conv(xm, conv0, p['m_conv_w'], p['m_conv_b'])
    xc = jax.nn.silu(xc)
    xch = xc.reshape(bsz, t, M_HEADS, M_HD)
    xmh = xm.reshape(bsz, t, M_HEADS, M_HD)
    q = _f32(jnp.einsum('bthd,hde->bthe', xch, p['m_wq'])) * (M_HD ** -0.5)
    k = _f32(jnp.einsum('bthd,hde->bthe', xch, p['m_wk']))
    v = _f32(jnp.einsum('bthd,hde->bthe', xmh, p['m_wv']))
    hm, c_new, n_new, m_new = _mlstm_chunkwise(
        q, k, v, _f32(ig) + _f32(p['m_ig_b']), _f32(fg) + _f32(p['m_fg_b']),
        _f32(c0), _f32(n0), _f32(m0))
    hm = jax.nn.sigmoid(_f32(og)).reshape(bsz, t, M_HEADS, M_HD) * hm
    hm = _head_norm(hm, LN_EPS).reshape(bsz, t, M_WIDTH) * _f32(p['m_norm_g'])
    hm = hm.astype(dt_) + p['m_skip'] * xc
    y_m = (hm * jax.nn.silu(zm)) @ p['w_bm']

    prev = jnp.concatenate([shift0[:, None].astype(dt_), rcols[:, :-1]], axis=1)
    shift_new = rcols[:, -1]
    xr = rcols + p['r_mu'] * (prev - rcols)
    r, kr, vr, wd, ad = jnp.split(xr, [R_WIDTH, 2 * R_WIDTH, 3 * R_WIDTH, 3 * R_WIDTH + R_LORA_W], axis=-1)
    heads = lambda a: a.reshape(bsz, t, R_HEADS, R_HD)
    wlog = -jax.nn.softplus(-_f32(p['r_w0'] + jnp.tanh(wd) @ p['r_w2'])) - 0.5
    decay = jnp.exp(-jnp.exp(wlog))
    a = jax.nn.sigmoid(_f32(p['r_a0'] + ad @ p['r_a2']))
    kk = heads(_f32(kr * p['r_k_k']))
    kk = kk / jnp.maximum(jnp.sqrt(jnp.sum(kk * kk, -1, keepdims=True)), 1e-12)
    kr = _f32(kr) * (1.0 + (a - 1.0) * _f32(p['r_k_a']))
    rh, kh, vh = heads(_f32(r)), heads(kr), heads(_f32(vr))
    yr, wkv_new = _rwkv7_recurrence(rh, heads(decay), kh, vh, kk, heads(a), _f32(wkv0))
    yr = _head_norm(yr, RWKV_GN_EPS).reshape(bsz, t, R_WIDTH) * _f32(p['r_ln_g']) + _f32(p['r_ln_b'])
    bonus = jnp.sum(rh * kh * _f32(p['r_r_k']), -1, keepdims=True) * vh
    yr = yr + bonus.reshape(bsz, t, R_WIDTH)
    y_r = (yr.astype(dt_) * jax.nn.silu(zr)) @ p['w_br']

    uf = _f32(u)
    s_re, s_im = _s5_scan(uf.reshape(bsz, t, S_GROUPS, S_GROUP), _f32(sre0), _f32(sim0),
                          _f32(p['s_lam_re']), _f32(p['s_lam_im']), _f32(p['s_log_dt']),
                          _f32(p['s_b_re']), _f32(p['s_b_im']))
    ys = (jnp.einsum('gcp,btgp->btgc', _f32(p['s_c_re']), s_re)
          - jnp.einsum('gcp,btgp->btgc', _f32(p['s_c_im']), s_im))
    ys = ys.reshape(bsz, t, S_WIDTH) + _f32(p['s_d']) * uf
    ys = jax.nn.gelu(ys).astype(dt_)
    ys = ys * jax.nn.sigmoid(ys @ p['s_glu_w'] + p['s_glu_b'])
    y_s = (ys * jax.nn.silu(zs)) @ p['w_bs']

    g = jax.nn.sigmoid(gm).reshape(bsz, t, N_BRANCH, D_MODEL)
    merged = g[:, :, 0] * y_m + g[:, :, 1] * y_r + g[:, :, 2] * y_s
    x_new = _layer_norm(ALPHA * x + merged @ p['w_out'], p['ln_g'], p['ln_b'])
    new_state = (c_new, n_new, m_new, conv_new, wkv_new, shift_new, s_re[:, -1], s_im[:, -1])
    return x_new, new_state


def setup_inputs(seed: int = 0) -> dict:
    key = jax.random.key(seed)
    ks = iter(jax.random.split(key, 64))
    f32 = jnp.float32

    def nrm(shape, scale):
        return scale * jax.random.normal(next(ks), shape, f32)

    def unif(shape, lo, hi):
        return jax.random.uniform(next(ks), shape, f32, lo, hi)

    L = DEPTH
    return {
        'x_prompt': nrm((BATCH, SEQ, D_MODEL), 1.0),
        'x_sample': nrm((DEC_BATCH, DEC_SEQ, D_MODEL), 1.0),
        'state_mlstm_c': nrm((L, DEC_BATCH, M_HEADS, M_HD, M_HD), 0.05),
        'state_mlstm_n': nrm((L, DEC_BATCH, M_HEADS, M_HD), 0.1),
        'state_mlstm_m': nrm((L, DEC_BATCH, M_HEADS), 1.0),
        'state_mlstm_conv': nrm((L, DEC_BATCH, CONV_W - 1, M_WIDTH), 1.0),
        'state_rwkv_wkv': nrm((L, DEC_BATCH, R_HEADS, R_HD, R_HD), 0.1),
        'state_rwkv_shift': nrm((L, DEC_BATCH, R_SHIFT_W), 1.0),
        'state_s5_re': nrm((L, DEC_BATCH, S_GROUPS, S_STATE), 0.1),
        'state_s5_im': nrm((L, DEC_BATCH, S_GROUPS, S_STATE), 0.1),
        'ln_in_g': 1.0 + nrm((D_MODEL,), 0.02),
        'ln_in_b': nrm((D_MODEL,), 0.02),
        'w_in': nrm((L, D_MODEL, N_IN), D_MODEL ** -0.5),
        'm_conv_w': nrm((L, CONV_W, M_WIDTH), CONV_W ** -0.5),
        'm_conv_b': nrm((L, M_WIDTH), 0.02),
        'm_wq': nrm((L, M_HEADS, M_HD, M_HD), M_HD ** -0.5),
        'm_wk': nrm((L, M_HEADS, M_HD, M_HD), M_HD ** -0.5),
        'm_wv': nrm((L, M_HEADS, M_HD, M_HD), M_HD ** -0.5),
        'm_ig_b': nrm((L, M_HEADS), 0.1),
        'm_fg_b': jnp.linspace(3.0, 6.0, M_HEADS) + nrm((L, M_HEADS), 0.1),
        'm_norm_g': 1.0 + nrm((L, M_WIDTH), 0.02),
        'm_skip': 1.0 + nrm((L, M_WIDTH), 0.02),
        'r_mu': unif((L, R_SHIFT_W), 0.0, 1.0),
        'r_w0': jnp.linspace(-6.0, -1.0, R_WIDTH) + nrm((L, R_WIDTH), 0.1),
        'r_w2': nrm((L, R_LORA_W, R_WIDTH), 0.5 * R_LORA_W ** -0.5),
        'r_a0': nrm((L, R_WIDTH), 0.1),
        'r_a2': nrm((L, R_LORA_A, R_WIDTH), 0.5 * R_LORA_A ** -0.5),
        'r_k_k': 0.85 + nrm((L, R_WIDTH), 0.02),
        'r_k_a': 1.0 + nrm((L, R_WIDTH), 0.02),
        'r_r_k': nrm((L, R_HEADS, R_HD), 0.1),
        'r_ln_g': 1.0 + nrm((L, R_WIDTH), 0.02),
        'r_ln_b': nrm((L, R_WIDTH), 0.02),
        's_lam_re': -0.5 + nrm((L, S_GROUPS, S_STATE), 0.01),
        's_lam_im': math.pi * jnp.arange(S_STATE, dtype=f32) + nrm((L, S_GROUPS, S_STATE), 0.01),
        's_log_dt': unif((L, S_GROUPS), math.log(1e-3), math.log(1e-1)),
        's_b_re': nrm((L, S_GROUPS, S_STATE, S_GROUP), (2.0 * S_GROUP) ** -0.5),
        's_b_im': nrm((L, S_GROUPS, S_STATE, S_GROUP), (2.0 * S_GROUP) ** -0.5),
        's_c_re': nrm((L, S_GROUPS, S_GROUP, S_STATE), (2.0 * S_STATE) ** -0.5),
        's_c_im': nrm((L, S_GROUPS, S_GROUP, S_STATE), (2.0 * S_STATE) ** -0.5),
        's_d': nrm((L, S_WIDTH), 1.0),
        's_glu_w': nrm((L, S_WIDTH, S_WIDTH), S_WIDTH ** -0.5),
        's_glu_b': nrm((L, S_WIDTH), 0.02),
        'w_bm': nrm((L, M_WIDTH, D_MODEL), BETA * M_WIDTH ** -0.5),
        'w_br': nrm((L, R_WIDTH, D_MODEL), BETA * R_WIDTH ** -0.5),
        'w_bs': nrm((L, S_WIDTH, D_MODEL), BETA * S_WIDTH ** -0.5),
        'w_out': nrm((L, D_MODEL, D_MODEL), BETA * D_MODEL ** -0.5),
        'ln_g': 1.0 + nrm((L, D_MODEL), 0.02),
        'ln_b': nrm((L, D_MODEL), 0.02),
    }


def reference(x_prompt, x_sample, state_mlstm_c, state_mlstm_n, state_mlstm_m, state_mlstm_conv,
              state_rwkv_wkv, state_rwkv_shift, state_s5_re, state_s5_im,
              ln_in_g, ln_in_b, w_in, m_conv_w, m_conv_b, m_wq, m_wk, m_wv, m_ig_b, m_fg_b,
              m_norm_g, m_skip, r_mu, r_w0, r_w2, r_a0, r_a2, r_k_k, r_k_a, r_r_k, r_ln_g, r_ln_b,
              s_lam_re, s_lam_im, s_log_dt, s_b_re, s_b_im, s_c_re, s_c_im, s_d, s_glu_w, s_glu_b,
              w_bm, w_br, w_bs, w_out, ln_g, ln_b):
    caches = (state_mlstm_c, state_mlstm_n, state_mlstm_m, state_mlstm_conv,
              state_rwkv_wkv, state_rwkv_shift, state_s5_re, state_s5_im)
    bp = x_prompt.shape[0]
    xp = _layer_norm(x_prompt, ln_in_g, ln_in_b)
    xs = _layer_norm(x_sample, ln_in_g, ln_in_b)
    new_p = [[] for _ in caches]
    new_s = [[] for _ in caches]
    for l in range(DEPTH):
        p = {'w_in': w_in[l], 'm_conv_w': m_conv_w[l], 'm_conv_b': m_conv_b[l],
             'm_wq': m_wq[l], 'm_wk': m_wk[l], 'm_wv': m_wv[l], 'm_ig_b': m_ig_b[l], 'm_fg_b': m_fg_b[l],
             'm_norm_g': m_norm_g[l], 'm_skip': m_skip[l], 'r_mu': r_mu[l], 'r_w0': r_w0[l],
             'r_w2': r_w2[l], 'r_a0': r_a0[l], 'r_a2': r_a2[l], 'r_k_k': r_k_k[l], 'r_k_a': r_k_a[l],
             'r_r_k': r_r_k[l], 'r_ln_g': r_ln_g[l], 'r_ln_b': r_ln_b[l], 's_lam_re': s_lam_re[l],
             's_lam_im': s_lam_im[l], 's_log_dt': s_log_dt[l], 's_b_re': s_b_re[l], 's_b_im': s_b_im[l],
             's_c_re': s_c_re[l], 's_c_im': s_c_im[l], 's_d': s_d[l], 's_glu_w': s_glu_w[l],
             's_glu_b': s_glu_b[l], 'w_bm': w_bm[l], 'w_br': w_br[l], 'w_bs': w_bs[l],
             'w_out': w_out[l], 'ln_g': ln_g[l], 'ln_b': ln_b[l]}
        fresh = (jnp.zeros((bp, M_HEADS, M_HD, M_HD), jnp.float32),
                 jnp.zeros((bp, M_HEADS, M_HD), jnp.float32),
                 jnp.full((bp, M_HEADS), NEG, jnp.float32),
                 jnp.zeros((bp, CONV_W - 1, M_WIDTH), x_prompt.dtype),
                 jnp.zeros((bp, R_HEADS, R_HD, R_HD), jnp.float32),
                 jnp.zeros((bp, R_SHIFT_W), x_prompt.dtype),
                 jnp.zeros((bp, S_GROUPS, S_STATE), jnp.float32),
                 jnp.zeros((bp, S_GROUPS, S_STATE), jnp.float32))
        xp, sp = _layer(xp, fresh, p)
        xs, ss = _layer(xs, tuple(c[l] for c in caches), p)
        for i in range(len(caches)):
            new_p[i].append(sp[i].astype(caches[i].dtype))
            new_s[i].append(ss[i].astype(caches[i].dtype))
    pc = [jnp.stack(a) for a in new_p]
    sc = [jnp.stack(a) for a in new_s]
    return (xp, xs, pc[0], sc[0], pc[1], sc[1], pc[2], sc[2], pc[3], sc[3],
            pc[4], sc[4], pc[5], sc[5], pc[6], sc[6], pc[7], sc[7])
```

```python
import functools
import math

import jax
import jax.numpy as jnp
import numpy as np
from jax import lax
from jax.experimental import pallas as pl
from jax.experimental.pallas import tpu as pltpu

D_MODEL = 1024
DEPTH = 4
M_HEADS = 4
M_HD = 256
CONV_W = 4
M_CHUNK = 128
R_HD = 64
R_HEADS = 16
R_LORA = 64
R_SHIFT_W = 3 * D_MODEL + 2 * R_LORA
S_GROUP = 16
S_GROUPS = 64
S_STATE = 64
ALPHA = (2.0 * DEPTH) ** 0.25
LN_EPS = 1e-5
RWKV_GN_EPS = 64e-5
NEG = -1e30

LANES = 128
Z_MAIN_W = 9 * D_MODEL
Z_R_W = R_SHIFT_W + LANES
VMEM_LIMIT = 56 * 1024 * 1024

BF16 = jnp.bfloat16
F32 = jnp.float32


def _cparams(sem):
    return pltpu.CompilerParams(dimension_semantics=sem, vmem_limit_bytes=VMEM_LIMIT)


def _dot(a, b):
    return jnp.dot(a.astype(BF16), b.astype(BF16), preferred_element_type=F32)


def _sigmoid(x):
    return 1.0 / (1.0 + jnp.exp(-x))


def _silu(x):
    return x * _sigmoid(x)


def _log_sigmoid(x):
    return -(jnp.maximum(-x, 0.0) + jnp.log1p(jnp.exp(-jnp.abs(x))))


def _gelu_tanh(x):
    return 0.5 * x * (1.0 + jnp.tanh(math.sqrt(2.0 / math.pi) * (x + 0.044715 * (x * x * x))))


def _ln_kernel(x_ref, g_ref, b_ref, o_ref):
    x = x_ref[...]
    mu = jnp.mean(x, -1, keepdims=True)
    xc = x - mu
    var = jnp.mean(xc * xc, -1, keepdims=True)
    o_ref[...] = xc * lax.rsqrt(var + LN_EPS) * g_ref[...] + b_ref[...]


def _layer_norm_rows(x, g, b):
    n = x.shape[0]
    tm = min(n, 1024)
    return pl.pallas_call(
        _ln_kernel,
        grid=(n // tm,),
        in_specs=[pl.BlockSpec((tm, D_MODEL), lambda i: (i, 0)),
                  pl.BlockSpec((1, D_MODEL), lambda i: (0, 0)),
                  pl.BlockSpec((1, D_MODEL), lambda i: (0, 0))],
        out_specs=pl.BlockSpec((tm, D_MODEL), lambda i: (i, 0)),
        out_shape=jax.ShapeDtypeStruct((n, D_MODEL), F32),
        compiler_params=_cparams(("arbitrary",)),
        name="ln_in",
    )(x, g.reshape(1, -1), b.reshape(1, -1))


def _mm_kernel(x_ref, w_ref, o_ref):
    o_ref[...] = jnp.dot(x_ref[...].astype(BF16), w_ref[...], preferred_element_type=F32)


def _matmul(x, w, tn, name):
    n, k = x.shape
    nw = w.shape[1]
    tm = min(n, 1024)
    return pl.pallas_call(
        _mm_kernel,
        grid=(n // tm, nw // tn),
        in_specs=[pl.BlockSpec((tm, k), lambda i, j: (i, 0)),
                  pl.BlockSpec((k, tn), lambda i, j: (0, j))],
        out_specs=pl.BlockSpec((tm, tn), lambda i, j: (i, j)),
        out_shape=jax.ShapeDtypeStruct((n, nw), F32),
        compiler_params=_cparams(("arbitrary", "arbitrary")),
        name=name,
    )(x, w)


def _put_halves(dst, x):
    for half in range(M_HD // LANES):
        dst[half] = x[:, half * LANES:(half + 1) * LANES]


def _get_rows(src, sl):
    return jnp.concatenate([src[half, sl, :] for half in range(M_HD // LANES)], axis=1)


def _mlstm_chunk_kernel(xm_ref, og_ref, zm_ref, gr_ref, cw_ref, cb_ref, wq_ref, wk_ref, wv_ref,
                        gb_ref, ng_ref, sk_ref,
                        am_ref, c_out, n_out, m_out, conv_out,
                        halo, xc_s, q_s, k_s, v_s, og_s, zm_s, o_s, c_s, n_s, m_s, *, bsz, chunk):
    ci = pl.program_id(1)
    rows = bsz * chunk
    nhalo = (CONV_W - 1) * bsz

    @pl.when(ci == 0)
    def _():
        halo[...] = jnp.zeros_like(halo)
        c_s[...] = jnp.zeros_like(c_s)
        n_s[...] = jnp.zeros_like(n_s)
        m_s[...] = jnp.full_like(m_s, NEG)

    xm = xm_ref[...]
    xp = jnp.concatenate([halo[...], xm], axis=0)
    acc = cb_ref[...] + cw_ref[CONV_W - 1:CONV_W, :] * xm
    for j in range(CONV_W - 1):
        acc = acc + cw_ref[j:j + 1, :] * xp[j * bsz:j * bsz + rows]
    xc = _silu(acc)
    halo[...] = xm[rows - nhalo:]
    xcb = xc.astype(BF16)
    _put_halves(xc_s, xc)
    _put_halves(q_s, jnp.dot(xcb, wq_ref[0], preferred_element_type=F32) * (M_HD ** -0.5))
    _put_halves(k_s, jnp.dot(xcb, wk_ref[0], preferred_element_type=F32))
    _put_halves(v_s, jnp.dot(xm.astype(BF16), wv_ref[0], preferred_element_type=F32))
    _put_halves(og_s, og_ref[...])
    _put_halves(zm_s, zm_ref[...])

    ig_r = gr_ref[0, 0] + gb_ref[0, :, 0:1]
    lf_r = _log_sigmoid(gr_ref[0, 1] + gb_ref[0, :, 1:2])
    tri = (lax.broadcasted_iota(jnp.int32, (chunk, chunk), 0)
           <= lax.broadcasted_iota(jnp.int32, (chunk, chunk), 1)).astype(F32)
    bc_r = jnp.dot(lf_r, tri, preferred_element_type=F32, precision=lax.Precision.HIGHEST)
    pad = jnp.zeros((chunk - 2 * bsz, chunk), F32)
    cols = jnp.concatenate([bc_r, ig_r, pad], axis=0).T
    causal = (lax.broadcasted_iota(jnp.int32, (chunk, chunk), 0)
              >= lax.broadcasted_iota(jnp.int32, (chunk, chunk), 1))

    for b in range(bsz):
        sl = pl.ds(b, chunk, stride=bsz)
        q = _get_rows(q_s, sl)
        k = _get_rows(k_s, sl)
        v = _get_rows(v_s, sl)
        b_c = cols[:, b:b + 1]
        i_c = cols[:, bsz + b:bsz + b + 1]
        b_r = bc_r[b:b + 1, :]
        i_r = ig_r[b:b + 1, :]
        m0 = m_s[b:b + 1, 0:1]
        c0 = c_s[b]
        n0 = n_s[b:b + 1, :]

        log_intra = jnp.where(causal, b_c - b_r + i_r, NEG)
        log_init = b_c + m0
        m_t = jnp.maximum(log_init, jnp.max(log_intra, axis=1, keepdims=True))
        w_intra = jnp.exp(log_intra - m_t)
        w_init = jnp.exp(log_init - m_t)
        qb = q.astype(BF16)
        s = lax.dot_general(qb, k.astype(BF16), (((1,), (1,)), ((), ())),
                            preferred_element_type=F32) * w_intra
        num = _dot(s, v) + w_init * jnp.dot(qb, c0.astype(BF16), preferred_element_type=F32)
        den = jnp.sum(s, axis=1, keepdims=True) + w_init * jnp.sum(q * n0, axis=1, keepdims=True)
        hc = num / jnp.maximum(jnp.abs(den), jnp.exp(-m_t))

        b_last = b_r[:, chunk - 1:chunk]
        m_new = jnp.maximum(b_last + m0, jnp.max(b_last - b_r + i_r, axis=1, keepdims=True))
        w_state = jnp.exp(b_last - b_c + i_c - m_new)
        decay = jnp.exp(b_last + m0 - m_new)
        kw = k * w_state
        c_s[b] = decay * c0 + lax.dot_general(kw.astype(BF16), v.astype(BF16), (((0,), (0,)), ((), ())),
                                              preferred_element_type=F32)
        n_s[b:b + 1, :] = decay * n0 + jnp.sum(kw, axis=0, keepdims=True)
        m_s[b:b + 1, :] = jnp.broadcast_to(m_new, (1, LANES))

        hm = _sigmoid(_get_rows(og_s, sl)) * hc
        mu = jnp.mean(hm, -1, keepdims=True)
        hd = hm - mu
        var = jnp.mean(hd * hd, -1, keepdims=True)
        hn = hd * lax.rsqrt(var + LN_EPS) * ng_ref[...] + sk_ref[...] * _get_rows(xc_s, sl)
        am = hn * _silu(_get_rows(zm_s, sl))
        for half in range(M_HD // LANES):
            o_s[half, sl, :] = am[:, half * LANES:(half + 1) * LANES]

    am_ref[...] = jnp.concatenate([o_s[half] for half in range(M_HD // LANES)], axis=1)

    @pl.when(ci == pl.num_programs(1) - 1)
    def _():
        c_out[:, 0] = c_s[...]
        n_out[...] = n_s[...]
        m_out[0] = m_s[...]
        conv_out[...] = halo[...]


def _mlstm_prompt(z_main, g_rows, p, bsz, t):
    chunk = min(M_CHUNK, t)
    rows = bsz * chunk
    nchunks = t // chunk
    kern = functools.partial(_mlstm_chunk_kernel, bsz=bsz, chunk=chunk)
    col = lambda off: (lambda h, c: (c, off + h))
    head_w = lambda h, c: (h, 0, 0)
    head_v = lambda h, c: (0, h)
    out_shapes = (
        jax.ShapeDtypeStruct((t * bsz, D_MODEL), F32),
        jax.ShapeDtypeStruct((bsz, M_HEADS, M_HD, M_HD), F32),
        jax.ShapeDtypeStruct((bsz, D_MODEL), F32),
        jax.ShapeDtypeStruct((M_HEADS, bsz, LANES), F32),
        jax.ShapeDtypeStruct(((CONV_W - 1) * bsz, D_MODEL), F32),
    )
    return pl.pallas_call(
        kern,
        grid=(M_HEADS, nchunks),
        in_specs=[
            pl.BlockSpec((rows, M_HD), col(0)),
            pl.BlockSpec((rows, M_HD), col(4)),
            pl.BlockSpec((rows, M_HD), col(8)),
            pl.BlockSpec((1, 2, bsz, chunk), lambda h, c: (h, 0, 0, c)),
            pl.BlockSpec((CONV_W, M_HD), head_v),
            pl.BlockSpec((1, M_HD), head_v),
            pl.BlockSpec((1, M_HD, M_HD), head_w),
            pl.BlockSpec((1, M_HD, M_HD), head_w),
            pl.BlockSpec((1, M_HD, M_HD), head_w),
            pl.BlockSpec((1, bsz, 2), head_w),
            pl.BlockSpec((1, M_HD), head_v),
            pl.BlockSpec((1, M_HD), head_v),
        ],
        out_specs=(
            pl.BlockSpec((rows, M_HD), lambda h, c: (c, h)),
            pl.BlockSpec((bsz, 1, M_HD, M_HD), lambda h, c: (0, h, 0, 0)),
            pl.BlockSpec((bsz, M_HD), head_v),
            pl.BlockSpec((1, bsz, LANES), head_w),
            pl.BlockSpec(((CONV_W - 1) * bsz, M_HD), head_v),
        ),
        out_shape=out_shapes,
        scratch_shapes=[
            pltpu.VMEM(((CONV_W - 1) * bsz, M_HD), F32),
        ] + [pltpu.VMEM((M_HD // LANES, rows, LANES), F32)] * 7 + [
            pltpu.VMEM((bsz, M_HD, M_HD), F32),
            pltpu.VMEM((bsz, M_HD), F32),
            pltpu.VMEM((bsz, LANES), F32),
        ],
        compiler_params=_cparams(("arbitrary", "arbitrary")),
        name="mlstm_prompt",
    )(z_main, z_main, z_main, g_rows, p['m_conv_w'], p['m_conv_b'].reshape(1, -1),
      p['m_wq'].astype(BF16), p['m_wk'].astype(BF16), p['m_wv'].astype(BF16),
      jnp.broadcast_to(jnp.stack([p['m_ig_b'], p['m_fg_b']], -1)[:, None, :], (M_HEADS, bsz, 2)),
      p['m_norm_g'].reshape(1, -1), p['m_skip'].reshape(1, -1))


DEC_GROUP = 8


def _mlstm_step_kernel(xm_ref, og_ref, zm_ref, gd_ref, cs0_ref, cs1_ref, cs2_ref, c0_ref, n0_ref,
                       cw_ref, cb_ref, wq_ref, wk_ref, wv_ref, wqt_ref, wkt_ref, ng_ref, sk_ref,
                       am_ref, c_out, n_out, m_out):
    xm = xm_ref[...]
    acc = cb_ref[...] + cw_ref[CONV_W - 1:CONV_W, :] * xm
    for j, cs_ref in enumerate((cs0_ref, cs1_ref, cs2_ref)):
        acc = acc + cw_ref[j:j + 1, :] * cs_ref[...]
    xc = _silu(acc)

    xcb = xc.astype(BF16)
    scale = M_HD ** -0.5
    nt = (((1,), (1,)), ((), ()))
    q_r = jnp.dot(xcb, wq_ref[0], preferred_element_type=F32) * scale
    k_r = jnp.dot(xcb, wk_ref[0], preferred_element_type=F32)
    v_r = jnp.dot(xm.astype(BF16), wv_ref[0], preferred_element_type=F32)
    q_t = lax.dot_general(wqt_ref[0], xcb, nt, preferred_element_type=F32) * scale
    k_t = lax.dot_general(wkt_ref[0], xcb, nt, preferred_element_type=F32)

    g = gd_ref[0]
    ig = g[:, 0:1]
    lf = _log_sigmoid(g[:, 1:2])
    m0 = g[:, 2:3]
    m_t = jnp.maximum(lf + m0, ig)
    w_in = jnp.exp(ig - m_t)
    w_st = jnp.exp(lf + m0 - m_t)
    n0 = n0_ref[...]
    qk = jnp.sum(q_r * k_r, axis=1, keepdims=True)
    s = qk * w_in
    den = s + w_st * jnp.sum(q_r * n0, axis=1, keepdims=True)
    n_out[...] = w_st * n0 + w_in * k_r
    m_out[0] = jnp.broadcast_to(m_t, (DEC_GROUP, 8))
    inv = 1.0 / jnp.maximum(jnp.abs(den), jnp.exp(-m_t))

    hs = []
    for b in range(DEC_GROUP):
        c0 = c0_ref[b, 0]
        qc = q_t[:, b:b + 1]
        kc = k_t[:, b:b + 1]
        vr = v_r[b:b + 1, :]
        qcm = jnp.sum(qc * c0, axis=0, keepdims=True)
        num = s[b:b + 1, :] * vr + w_st[b:b + 1, :] * qcm
        hs.append(num * inv[b:b + 1, :])
        c_out[b, 0] = w_st[b:b + 1, :] * c0 + (w_in[b:b + 1, :] * kc) * vr
    hc = jnp.concatenate(hs, axis=0)

    hm = _sigmoid(og_ref[...]) * hc
    mu = jnp.mean(hm, -1, keepdims=True)
    hd = hm - mu
    var = jnp.mean(hd * hd, -1, keepdims=True)
    hn = hd * lax.rsqrt(var + LN_EPS) * ng_ref[...] + sk_ref[...] * xc
    am_ref[...] = hn * _silu(zm_ref[...])


def _mlstm_step(z_main, gdec, conv0, c0, n0, p):
    n = z_main.shape[0]
    gsz = DEC_GROUP
    col = lambda off: (lambda i, h: (i, off + h))
    head_w = lambda i, h: (h, 0, 0)
    head_v = lambda i, h: (0, h)
    out_shapes = (
        jax.ShapeDtypeStruct((n, D_MODEL), F32),
        jax.ShapeDtypeStruct((n, M_HEADS, M_HD, M_HD), F32),
        jax.ShapeDtypeStruct((n, D_MODEL), F32),
        jax.ShapeDtypeStruct((M_HEADS, n, 8), F32),
    )
    conv_flat = conv0.reshape(n, (CONV_W - 1) * D_MODEL)
    wq = p['m_wq'].astype(BF16)
    wk = p['m_wk'].astype(BF16)
    return pl.pallas_call(
        _mlstm_step_kernel,
        grid=(n // gsz, M_HEADS),
        in_specs=[
            pl.BlockSpec((gsz, M_HD), col(0)),
            pl.BlockSpec((gsz, M_HD), col(4)),
            pl.BlockSpec((gsz, M_HD), col(8)),
            pl.BlockSpec((1, gsz, 8), lambda i, h: (h, i, 0)),
            pl.BlockSpec((gsz, M_HD), col(0)),
            pl.BlockSpec((gsz, M_HD), col(4)),
            pl.BlockSpec((gsz, M_HD), col(8)),
            pl.BlockSpec((gsz, 1, M_HD, M_HD), lambda i, h: (i, h, 0, 0)),
            pl.BlockSpec((gsz, M_HD), lambda i, h: (i, h)),
            pl.BlockSpec((CONV_W, M_HD), head_v),
            pl.BlockSpec((1, M_HD), head_v),
            pl.BlockSpec((1, M_HD, M_HD), head_w),
            pl.BlockSpec((1, M_HD, M_HD), head_w),
            pl.BlockSpec((1, M_HD, M_HD), head_w),
            pl.BlockSpec((1, M_HD, M_HD), head_w),
            pl.BlockSpec((1, M_HD, M_HD), head_w),
            pl.BlockSpec((1, M_HD), head_v),
            pl.BlockSpec((1, M_HD), head_v),
        ],
        out_specs=(
            pl.BlockSpec((gsz, M_HD), lambda i, h: (i, h)),
            pl.BlockSpec((gsz, 1, M_HD, M_HD), lambda i, h: (i, h, 0, 0)),
            pl.BlockSpec((gsz, M_HD), lambda i, h: (i, h)),
            pl.BlockSpec((1, gsz, 8), lambda i, h: (h, i, 0)),
        ),
        out_shape=out_shapes,
        compiler_params=_cparams(("arbitrary", "arbitrary")),
        name="mlstm_step",
    )(z_main, z_main, z_main, gdec, conv_flat, conv_flat, conv_flat, c0, n0,
      p['m_conv_w'], p['m_conv_b'].reshape(1, -1), wq, wk, p['m_wv'].astype(BF16),
      jnp.swapaxes(wq, 1, 2), jnp.swapaxes(wk, 1, 2),
      p['m_norm_g'].reshape(1, -1), p['m_skip'].reshape(1, -1))


def _rwkv_prep_kernel(zr_ref, sh_ref, mu_ref, w2a_ref, w0_ref, a0_ref,
                      r_out, k_out, v_out, d_out, a_out, halo, *, bsz):
    tm = zr_ref.shape[0]

    @pl.when(pl.program_id(0) == 0)
    def _():
        halo[...] = sh_ref[...]

    cur = zr_ref[:, :R_SHIFT_W]
    if tm > bsz:
        prev = jnp.concatenate([halo[...], cur[:tm - bsz]], axis=0)
    else:
        prev = halo[...]
    halo[...] = cur[tm - bsz:]
    xr = cur + mu_ref[...] * (prev - cur)
    r_out[...] = xr[:, :D_MODEL]
    k_out[...] = xr[:, D_MODEL:2 * D_MODEL]
    v_out[...] = xr[:, 2 * D_MODEL:3 * D_MODEL]
    la = xr[:, 3 * D_MODEL:]
    lane = lax.broadcasted_iota(jnp.int32, la.shape, 1)
    la = jnp.where(lane < R_LORA, jnp.tanh(la), la)
    lo = jnp.dot(la.astype(BF16), w2a_ref[...], preferred_element_type=F32)
    wl = w0_ref[...] + lo[:, :D_MODEL]
    wlog = _log_sigmoid(wl) - 0.5
    d_out[...] = jnp.exp(-jnp.exp(wlog))
    a_out[...] = _sigmoid(a0_ref[...] + lo[:, D_MODEL:])


def _rwkv_prep(z_r, shift0, p, bsz):
    n = z_r.shape[0]
    tm = min(n, 512)
    w2a = jnp.zeros((2 * R_LORA, 2 * D_MODEL), F32)
    w2a = w2a.at[:R_LORA, :D_MODEL].set(p['r_w2']).at[R_LORA:, D_MODEL:].set(p['r_a2']).astype(BF16)
    row = lambda i: (i, 0)
    fix = lambda i: (0, 0)
    out = jax.ShapeDtypeStruct((n, D_MODEL), F32)
    return pl.pallas_call(
        functools.partial(_rwkv_prep_kernel, bsz=bsz),
        grid=(n // tm,),
        in_specs=[pl.BlockSpec((tm, Z_R_W), row),
                  pl.BlockSpec((bsz, R_SHIFT_W), fix),
                  pl.BlockSpec((1, R_SHIFT_W), fix),
                  pl.BlockSpec((2 * R_LORA, 2 * D_MODEL), fix),
                  pl.BlockSpec((1, D_MODEL), fix),
                  pl.BlockSpec((1, D_MODEL), fix)],
        out_specs=tuple(pl.BlockSpec((tm, D_MODEL), row) for _ in range(5)),
        out_shape=(out,) * 5,
        scratch_shapes=[pltpu.VMEM((bsz, R_SHIFT_W), F32)],
        compiler_params=_cparams(("arbitrary",)),
        name="rwkv_prep",
    )(z_r, shift0, p['r_mu'].reshape(1, -1), w2a, p['r_w0'].reshape(1, -1), p['r_a0'].reshape(1, -1))


def _rwkv_scan_kernel(r_ref, d_ref, k_ref, v_ref, a_ref, kk_ref, ka_ref, rk_ref, g_ref, b_ref, s0_ref,
                      y_ref, s_out, s_s, y_s):
    ci = pl.program_id(1)
    tc = r_ref.shape[0]

    @pl.when(ci == 0)
    def _():
        s_s[...] = s0_ref[...]

    def step(t, carry):
        kr = k_ref[t]
        a = a_ref[t]
        r = r_ref[t]
        w = d_ref[t]
        v = v_ref[t]
        kk = kr * kk_ref[...]
        nrm = jnp.sqrt(jnp.sum(kk * kk, axis=0, keepdims=True))
        kk = kk / jnp.maximum(nrm, 1e-12)
        k = kr * (1.0 + (a - 1.0) * ka_ref[...])
        nkk = -kk
        kka = kk * a
        for i in range(R_HD):
            s_i = s_s[i]
            sa = jnp.sum(s_i * nkk, axis=0, keepdims=True)
            s_n = s_i * w + sa * kka + v[i:i + 1, :] * k
            s_s[i] = s_n
            y_s[i:i + 1, :] = jnp.sum(s_n * r, axis=0, keepdims=True)
        y = y_s[...]
        mu = jnp.mean(y, axis=0, keepdims=True)
        yc = y - mu
        var = jnp.mean(yc * yc, axis=0, keepdims=True)
        bonus = jnp.sum(r * k * rk_ref[...], axis=0, keepdims=True) * v
        y_ref[t] = yc * lax.rsqrt(var + RWKV_GN_EPS) * g_ref[...] + b_ref[...] + bonus
        return carry

    lax.fori_loop(0, tc, step, 0)

    @pl.when(ci == pl.num_programs(1) - 1)
    def _():
        s_out[...] = s_s[...]


def _rwkv_scan(r, d, k, v, a, pk, s0):
    t, _, nl = r.shape
    tc = min(t, 32)
    seq = pl.BlockSpec((tc, R_HD, LANES), lambda l, c: (c, 0, l))
    par = pl.BlockSpec((R_HD, LANES), lambda l, c: (0, l))
    st = pl.BlockSpec((R_HD, R_HD, LANES), lambda l, c: (0, 0, l))
    return pl.pallas_call(
        _rwkv_scan_kernel,
        grid=(nl // LANES, t // tc),
        in_specs=[seq] * 5 + [par] * 5 + [st],
        out_specs=(seq, st),
        out_shape=(jax.ShapeDtypeStruct((t, R_HD, nl), F32),
                   jax.ShapeDtypeStruct((R_HD, R_HD, nl), F32)),
        scratch_shapes=[pltpu.VMEM((R_HD, R_HD, LANES), F32), pltpu.VMEM((R_HD, LANES), F32)],
        compiler_params=_cparams(("arbitrary", "arbitrary")),
        name="rwkv_scan",
    )(r, d, k, v, a, *pk, s0)


def _to_lanes(x, t, bsz):
    return x.reshape(t, bsz, R_HEADS, R_HD).transpose(0, 3, 1, 2).reshape(t, R_HD, bsz * R_HEADS)


def _from_lanes(y, t, bsz):
    return y.reshape(t, R_HD, bsz, R_HEADS).transpose(0, 2, 3, 1).reshape(t * bsz, D_MODEL)


def _param_lanes(pv, bsz):
    pt = pv.reshape(R_HEADS, R_HD).T
    return jnp.broadcast_to(pt[:, None, :], (R_HD, bsz, R_HEADS)).reshape(R_HD, bsz * R_HEADS)


S5_COLS = 1024
S5_CH = S5_COLS // S_STATE * S_GROUP


def _s5_kernel(u_ref, bre_ref, bim_ref, cre_ref, cim_ref, lre_ref, lim_ref, d_ref, s0re_ref, s0im_ref,
               y_ref, sre_out, sim_out, xre, xim, sre, sim, *, bsz):
    ci = pl.program_id(1)
    tc = u_ref.shape[0] // bsz

    @pl.when(ci == 0)
    def _():
        sre[...] = s0re_ref[...]
        sim[...] = s0im_ref[...]

    u = u_ref[...]
    ub = u.astype(BF16)
    xre[...] = jnp.dot(ub, bre_ref[0], preferred_element_type=F32)
    xim[...] = jnp.dot(ub, bim_ref[0], preferred_element_type=F32)
    lre = jnp.broadcast_to(lre_ref[...], (bsz, S5_COLS))
    lim = jnp.broadcast_to(lim_ref[...], (bsz, S5_COLS))

    def step(t, carry):
        s_re, s_im = carry
        rows = pl.ds(pl.multiple_of(t * bsz, bsz), bsz)
        n_re = lre * s_re - lim * s_im + xre[rows, :]
        n_im = lre * s_im + lim * s_re + xim[rows, :]
        xre[rows, :] = n_re
        xim[rows, :] = n_im
        return n_re, n_im

    f_re, f_im = lax.fori_loop(0, tc, step, (sre[...], sim[...]))
    sre[...] = f_re
    sim[...] = f_im
    ys = (jnp.dot(xre[...].astype(BF16), cre_ref[0], preferred_element_type=F32)
          - jnp.dot(xim[...].astype(BF16), cim_ref[0], preferred_element_type=F32))
    y_ref[...] = _gelu_tanh(ys + d_ref[...] * u)

    @pl.when(ci == pl.num_programs(1) - 1)
    def _():
        sre_out[...] = sre[...]
        sim_out[...] = sim[...]


def _s5_weights(p):
    lam_re = jnp.minimum(p['s_lam_re'], -1e-4)
    lam_im = p['s_lam_im']
    dt = jnp.exp(p['s_log_dt'])[:, None]
    mag = jnp.exp(lam_re * dt)
    lb_re = mag * jnp.cos(lam_im * dt)
    lb_im = mag * jnp.sin(lam_im * dt)
    den = lam_re * lam_re + lam_im * lam_im
    nr = lb_re - 1.0
    coef_re = (nr * lam_re + lb_im * lam_im) / den
    coef_im = (lb_im * lam_re - nr * lam_im) / den
    b_re, b_im = p['s_b_re'], p['s_b_im']
    bb_re = coef_re[..., None] * b_re - coef_im[..., None] * b_im
    bb_im = coef_re[..., None] * b_im + coef_im[..., None] * b_re
    nblk = S_GROUPS * S_STATE // S5_COLS
    gpb = S_GROUPS // nblk
    eye = jnp.eye(gpb, dtype=F32)

    def pack_b(bb):
        x = bb.reshape(nblk, gpb, S_STATE, S_GROUP)
        return jnp.einsum('ngpc,gh->ngchp', x, eye).reshape(nblk, gpb * S_GROUP, gpb * S_STATE).astype(BF16)

    def pack_c(cc):
        x = cc.reshape(nblk, gpb, S_GROUP, S_STATE)
        return jnp.einsum('ngcp,gh->ngphc', x, eye).reshape(nblk, gpb * S_STATE, gpb * S_GROUP).astype(BF16)

    return (pack_b(bb_re), pack_b(bb_im), pack_c(p['s_c_re']), pack_c(p['s_c_im']),
            lb_re.reshape(1, -1), lb_im.reshape(1, -1))


def _s5(z_main, s0re, s0im, p, bsz, t):
    n = z_main.shape[0]
    tc = min(t, 128)
    rows = tc * bsz
    nblk = S_GROUPS * S_STATE // S5_COLS
    bre, bim, cre, cim, lre, lim = _s5_weights(p)
    u_col0 = 4 * D_MODEL // S5_CH
    blk = lambda j, c: (j, 0, 0)
    cols = lambda j, c: (0, j)
    return pl.pallas_call(
        functools.partial(_s5_kernel, bsz=bsz),
        grid=(nblk, t // tc),
        in_specs=[pl.BlockSpec((rows, S5_CH), lambda j, c: (c, u_col0 + j)),
                  pl.BlockSpec((1, S5_CH, S5_COLS), blk),
                  pl.BlockSpec((1, S5_CH, S5_COLS), blk),
                  pl.BlockSpec((1, S5_COLS, S5_CH), blk),
                  pl.BlockSpec((1, S5_COLS, S5_CH), blk),
                  pl.BlockSpec((1, S5_COLS), cols),
                  pl.BlockSpec((1, S5_COLS), cols),
                  pl.BlockSpec((1, S5_CH), cols),
                  pl.BlockSpec((bsz, S5_COLS), cols),
                  pl.BlockSpec((bsz, S5_COLS), cols)],
        out_specs=(pl.BlockSpec((rows, S5_CH), lambda j, c: (c, j)),
                   pl.BlockSpec((bsz, S5_COLS), cols),
                   pl.BlockSpec((bsz, S5_COLS), cols)),
        out_shape=(jax.ShapeDtypeStruct((n, D_MODEL), F32),
                   jax.ShapeDtypeStruct((bsz, S_GROUPS * S_STATE), F32),
                   jax.ShapeDtypeStruct((bsz, S_GROUPS * S_STATE), F32)),
        scratch_shapes=[pltpu.VMEM((rows, S5_COLS), F32), pltpu.VMEM((rows, S5_COLS), F32),
                        pltpu.VMEM((bsz, S5_COLS), F32), pltpu.VMEM((bsz, S5_COLS), F32)],
        compiler_params=_cparams(("arbitrary", "arbitrary")),
        name="s5_scan",
    )(z_main, bre, bim, cre, cim, lre, lim, p['s_d'].reshape(1, -1), s0re, s0im)


def _merge_kernel(x_ref, am_ref, yr_ref, ys_ref, zr_ref, zs_ref, g0_ref, g1_ref, g2_ref,
                  wbm_ref, wbr_ref, wbs_ref, wglu_ref, bglu_ref, wout_ref, lg_ref, lb_ref, o_ref):
    y_m = _dot(am_ref[...], wbm_ref[...])
    y_r = _dot(yr_ref[...] * _silu(zr_ref[...]), wbr_ref[...])
    ys = ys_ref[...]
    gl = ys * _sigmoid(_dot(ys, wglu_ref[...]) + bglu_ref[...])
    y_s = _dot(gl * _silu(zs_ref[...]), wbs_ref[...])
    merged = _sigmoid(g0_ref[...]) * y_m + _sigmoid(g1_ref[...]) * y_r + _sigmoid(g2_ref[...]) * y_s
    h = ALPHA * x_ref[...] + _dot(merged, wout_ref[...])
    mu = jnp.mean(h, -1, keepdims=True)
    hc = h - mu
    var = jnp.mean(hc * hc, -1, keepdims=True)
    o_ref[...] = hc * lax.rsqrt(var + LN_EPS) * lg_ref[...] + lb_ref[...]


def _merge(x, am, yr, ys, z_main, p):
    n = x.shape[0]
    tm = min(n, 256)
    row = lambda i: (i, 0)
    zcol = lambda c: (lambda i: (i, c))
    fix = lambda i: (0, 0)
    act = pl.BlockSpec((tm, D_MODEL), row)
    wsp = pl.BlockSpec((D_MODEL, D_MODEL), fix)
    vsp = pl.BlockSpec((1, D_MODEL), fix)
    return pl.pallas_call(
        _merge_kernel,
        grid=(n // tm,),
        in_specs=[act, act, act, act,
                  pl.BlockSpec((tm, D_MODEL), zcol(3)), pl.BlockSpec((tm, D_MODEL), zcol(5)),
                  pl.BlockSpec((tm, D_MODEL), zcol(6)), pl.BlockSpec((tm, D_MODEL), zcol(7)),
                  pl.BlockSpec((tm, D_MODEL), zcol(8)),
                  wsp, wsp, wsp, wsp, vsp, wsp, vsp, vsp],
        out_specs=act,
        out_shape=jax.ShapeDtypeStruct((n, D_MODEL), F32),
        compiler_params=_cparams(("arbitrary",)),
        name="merge_out",
    )(x, am, yr, ys, z_main, z_main, z_main, z_main, z_main,
      p['w_bm'].astype(BF16), p['w_br'].astype(BF16), p['w_bs'].astype(BF16),
      p['s_glu_w'].astype(BF16), p['s_glu_b'].reshape(1, -1), p['w_out'].astype(BF16),
      p['ln_g'].reshape(1, -1), p['ln_b'].reshape(1, -1))


def _split_w_in(w_in):
    offs = np.cumsum([0, D_MODEL, M_HEADS, M_HEADS, D_MODEL, D_MODEL, R_SHIFT_W, D_MODEL, D_MODEL, D_MODEL,
                      3 * D_MODEL])
    seg = lambda i: w_in[:, offs[i]:offs[i + 1]]
    xm, ig, fg, og, zm, rc, zr, u, zs, gm = (seg(i) for i in range(10))
    w_main = jnp.concatenate([xm, og, zm, zr, u, zs, gm], axis=1).astype(BF16)
    pad = jnp.zeros((D_MODEL, LANES - 2 * M_HEADS), w_in.dtype)
    w_r = jnp.concatenate([rc, ig, fg, pad], axis=1).astype(BF16)
    return w_main, w_r


def _layer(x, state, p, bsz, t):
    n = t * bsz
    w_main, w_r = _split_w_in(p['w_in'])
    z_main = _matmul(x, w_main, 1536, "proj_main")
    z_r = _matmul(x, w_r, Z_R_W // 2, "proj_shift")
    gates = z_r[:, R_SHIFT_W:R_SHIFT_W + 2 * M_HEADS]

    if state is None:
        chunk = min(M_CHUNK, t)
        g_rows = gates.reshape(t, bsz, 2, M_HEADS).transpose(3, 2, 1, 0)
        am, c_new, n_new, m_new, conv_new = _mlstm_prompt(z_main, g_rows, p, bsz, t)
        m_new = m_new[:, :, 0].T
        conv_new = conv_new.reshape(CONV_W - 1, bsz, D_MODEL).transpose(1, 0, 2)
    else:
        c0, n0, m0, conv0 = state[:4]
        bias = jnp.stack([p['m_ig_b'], p['m_fg_b']], 0)
        gd = gates.reshape(n, 2, M_HEADS) + bias[None]
        gd = jnp.concatenate([gd, m0[:, None, :], jnp.zeros((n, 5, M_HEADS), F32)], axis=1).transpose(2, 0, 1)
        am, c_new, n_new, m_new = _mlstm_step(z_main, gd, conv0, c0, n0.reshape(n, D_MODEL), p)
        m_new = m_new[:, :, 0].T
        conv_new = jnp.concatenate([conv0[:, 1:], z_main[:, None, :D_MODEL]], axis=1)
    n_new = n_new.reshape(bsz, M_HEADS, M_HD)

    shift0 = jnp.zeros((bsz, R_SHIFT_W), F32) if state is None else state[5]
    r, kr, vr, dec, a = _rwkv_prep(z_r, shift0, p, bsz)
    nl = bsz * R_HEADS
    if state is None:
        s0 = jnp.zeros((R_HD, R_HD, nl), F32)
    else:
        s0 = state[4].transpose(2, 3, 0, 1).reshape(R_HD, R_HD, nl)
    pk = tuple(_param_lanes(p[k], bsz) for k in ('r_k_k', 'r_k_a', 'r_r_k', 'r_ln_g', 'r_ln_b'))
    yr_l, s_new = _rwkv_scan(*(_to_lanes(v, t, bsz) for v in (r, dec, kr, vr, a)), pk, s0)
    yr = _from_lanes(yr_l, t, bsz)
    wkv_new = s_new.reshape(R_HD, R_HD, bsz, R_HEADS).transpose(2, 3, 0, 1)
    shift_new = z_r[n - bsz:, :R_SHIFT_W]

    if state is None:
        s0re = jnp.zeros((bsz, S_GROUPS * S_STATE), F32)
        s0im = s0re
    else:
        s0re = state[6].reshape(bsz, -1)
        s0im = state[7].reshape(bsz, -1)
    ys, sre_new, sim_new = _s5(z_main, s0re, s0im, p, bsz, t)

    x_new = _merge(x, am, yr, ys, z_main, p)
    new_state = (c_new, n_new, m_new, conv_new, wkv_new, shift_new,
                 sre_new.reshape(bsz, S_GROUPS, S_STATE), sim_new.reshape(bsz, S_GROUPS, S_STATE))
    return x_new, new_state


_PARAM_NAMES = ('w_in', 'm_conv_w', 'm_conv_b', 'm_wq', 'm_wk', 'm_wv', 'm_ig_b', 'm_fg_b', 'm_norm_g', 'm_skip',
                'r_mu', 'r_w0', 'r_w2', 'r_a0', 'r_a2', 'r_k_k', 'r_k_a', 'r_r_k', 'r_ln_g', 'r_ln_b',
                's_lam_re', 's_lam_im', 's_log_dt', 's_b_re', 's_b_im', 's_c_re', 's_c_im', 's_d',
                's_glu_w', 's_glu_b', 'w_bm', 'w_br', 'w_bs', 'w_out', 'ln_g', 'ln_b')


def kernel(x_prompt, x_sample, state_mlstm_c, state_mlstm_n, state_mlstm_m, state_mlstm_conv, state_rwkv_wkv, state_rwkv_shift, state_s5_re, state_s5_im, ln_in_g, ln_in_b, w_in, m_conv_w, m_conv_b, m_wq, m_wk, m_wv, m_ig_b, m_fg_b, m_norm_g, m_skip, r_mu, r_w0, r_w2, r_a0, r_a2, r_k_k, r_k_a, r_r_k, r_ln_g, r_ln_b, s_lam_re, s_lam_im, s_log_dt, s_b_re, s_b_im, s_c_re, s_c_im, s_d, s_glu_w, s_glu_b, w_bm, w_br, w_bs, w_out, ln_g, ln_b):
    weights = dict(zip(_PARAM_NAMES, (w_in, m_conv_w, m_conv_b, m_wq, m_wk, m_wv, m_ig_b, m_fg_b, m_norm_g, m_skip,
                                      r_mu, r_w0, r_w2, r_a0, r_a2, r_k_k, r_k_a, r_r_k, r_ln_g, r_ln_b,
                                      s_lam_re, s_lam_im, s_log_dt, s_b_re, s_b_im, s_c_re, s_c_im, s_d,
                                      s_glu_w, s_glu_b, w_bm, w_br, w_bs, w_out, ln_g, ln_b)))
    caches = (state_mlstm_c, state_mlstm_n, state_mlstm_m, state_mlstm_conv,
              state_rwkv_wkv, state_rwkv_shift, state_s5_re, state_s5_im)
    bp, tp, _ = x_prompt.shape
    bs, ts, _ = x_sample.shape
    xp = _layer_norm_rows(x_prompt.transpose(1, 0, 2).reshape(tp * bp, D_MODEL), ln_in_g, ln_in_b)
    xs = _layer_norm_rows(x_sample.transpose(1, 0, 2).reshape(ts * bs, D_MODEL), ln_in_g, ln_in_b)
    new_p = [[] for _ in caches]
    new_s = [[] for _ in caches]
    for l in range(DEPTH):
        p = {k: v[l] for k, v in weights.items()}
        xp, sp = _layer(xp, None, p, bp, tp)
        xs, ss = _layer(xs, tuple(c[l] for c in caches), p, bs, ts)
        for i in range(len(caches)):
            new_p[i].append(sp[i])
            new_s[i].append(ss[i])
    pc = [jnp.stack(a) for a in new_p]
    sc = [jnp.stack(a) for a in new_s]
    yp = xp.reshape(tp, bp, D_MODEL).transpose(1, 0, 2)
    ys = xs.reshape(ts, bs, D_MODEL).transpose(1, 0, 2)
    return (yp, ys, pc[0], sc[0], pc[1], sc[1], pc[2], sc[2], pc[3], sc[3],
            pc[4], sc[4], pc[5], sc[5], pc[6], sc[6], pc[7], sc[7])
```

```python
import functools
import math

import jax
import jax.numpy as jnp
import numpy as np
from jax import lax
from jax.experimental import pallas as pl
from jax.experimental.pallas import tpu as pltpu

D_MODEL = 1024
DEPTH = 4
M_HEADS = 4
M_HD = 256
CONV_W = 4
M_CHUNK = 128
R_HD = 64
R_HEADS = 16
R_LORA = 64
R_SHIFT_W = 3 * D_MODEL + 2 * R_LORA
S_GROUP = 16
S_GROUPS = 64
S_STATE = 64
ALPHA = (2.0 * DEPTH) ** 0.25
LN_EPS = 1e-5
RWKV_GN_EPS = 64e-5
NEG = -1e30

LANES = 128
Z_MAIN_W = 9 * D_MODEL
Z_R_W = R_SHIFT_W + LANES
VMEM_LIMIT = 56 * 1024 * 1024

BF16 = jnp.bfloat16
F32 = jnp.float32


def _cparams(sem):
    return pltpu.CompilerParams(dimension_semantics=sem, vmem_limit_bytes=VMEM_LIMIT)


def _dot(a, b):
    return jnp.dot(a.astype(BF16), b.astype(BF16), preferred_element_type=F32)


def _sigmoid(x):
    return 1.0 / (1.0 + jnp.exp(-x))


def _silu(x):
    return x * _sigmoid(x)


def _log_sigmoid(x):
    return -(jnp.maximum(-x, 0.0) + jnp.log1p(jnp.exp(-jnp.abs(x))))


def _gelu_tanh(x):
    return 0.5 * x * (1.0 + jnp.tanh(math.sqrt(2.0 / math.pi) * (x + 0.044715 * (x * x * x))))


def _ln_kernel(x_ref, g_ref, b_ref, o_ref):
    x = x_ref[...]
    mu = jnp.mean(x, -1, keepdims=True)
    xc = x - mu
    var = jnp.mean(xc * xc, -1, keepdims=True)
    o_ref[...] = xc * lax.rsqrt(var + LN_EPS) * g_ref[...] + b_ref[...]


def _layer_norm_rows(x, g, b):
    n = x.shape[0]
    tm = min(n, 1024)
    return pl.pallas_call(
        _ln_kernel,
        grid=(n // tm,),
        in_specs=[pl.BlockSpec((tm, D_MODEL), lambda i: (i, 0)),
                  pl.BlockSpec((1, D_MODEL), lambda i: (0, 0)),
                  pl.BlockSpec((1, D_MODEL), lambda i: (0, 0))],
        out_specs=pl.BlockSpec((tm, D_MODEL), lambda i: (i, 0)),
        out_shape=jax.ShapeDtypeStruct((n, D_MODEL), F32),
        compiler_params=_cparams(("arbitrary",)),
        name="ln_in",
    )(x, g.reshape(1, -1), b.reshape(1, -1))


def _mm_kernel(x_ref, w_ref, o_ref):
    o_ref[...] = jnp.dot(x_ref[...].astype(BF16), w_ref[...], preferred_element_type=F32)


def _matmul(x, w, tn, name):
    n, k = x.shape
    nw = w.shape[1]
    tm = min(n, 1024)
    return pl.pallas_call(
        _mm_kernel,
        grid=(n // tm, nw // tn),
        in_specs=[pl.BlockSpec((tm, k), lambda i, j: (i, 0)),
                  pl.BlockSpec((k, tn), lambda i, j: (0, j))],
        out_specs=pl.BlockSpec((tm, tn), lambda i, j: (i, j)),
        out_shape=jax.ShapeDtypeStruct((n, nw), F32),
        compiler_params=_cparams(("arbitrary", "arbitrary")),
        name=name,
    )(x, w)


def _put_halves(dst, x):
    for half in range(M_HD // LANES):
        dst[half] = x[:, half * LANES:(half + 1) * LANES]


def _get_rows(src, sl):
    return jnp.concatenate([src[half, sl, :] for half in range(M_HD // LANES)], axis=1)


def _mlstm_chunk_kernel(xm_ref, og_ref, zm_ref, gr_ref, cw_ref, cb_ref, wq_ref, wk_ref, wv_ref,
                        gb_ref, ng_ref, sk_ref,
                        am_ref, c_out, n_out, m_out, conv_out,
                        halo, xc_s, q_s, k_s, v_s, og_s, zm_s, o_s, c_s, n_s, m_s, *, bsz, chunk):
    ci = pl.program_id(1)
    rows = bsz * chunk
    nhalo = (CONV_W - 1) * bsz

    @pl.when(ci == 0)
    def _():
        halo[...] = jnp.zeros_like(halo)
        c_s[...] = jnp.zeros_like(c_s)
        n_s[...] = jnp.zeros_like(n_s)
        m_s[...] = jnp.full_like(m_s, NEG)

    xm = xm_ref[...]
    xp = jnp.concatenate([halo[...], xm], axis=0)
    acc = cb_ref[...] + cw_ref[CONV_W - 1:CONV_W, :] * xm
    for j in range(CONV_W - 1):
        acc = acc + cw_ref[j:j + 1, :] * xp[j * bsz:j * bsz + rows]
    xc = _silu(acc)
    halo[...] = xm[rows - nhalo:]
    xcb = xc.astype(BF16)
    _put_halves(xc_s, xc)
    _put_halves(q_s, jnp.dot(xcb, wq_ref[0], preferred_element_type=F32) * (M_HD ** -0.5))
    _put_halves(k_s, jnp.dot(xcb, wk_ref[0], preferred_element_type=F32))
    _put_halves(v_s, jnp.dot(xm.astype(BF16), wv_ref[0], preferred_element_type=F32))
    _put_halves(og_s, og_ref[...])
    _put_halves(zm_s, zm_ref[...])

    ig_r = gr_ref[0, 0] + gb_ref[0, :, 0:1]
    lf_r = _log_sigmoid(gr_ref[0, 1] + gb_ref[0, :, 1:2])
    tri = (lax.broadcasted_iota(jnp.int32, (chunk, chunk), 0)
           <= lax.broadcasted_iota(jnp.int32, (chunk, chunk), 1)).astype(F32)
    bc_r = jnp.dot(lf_r, tri, preferred_element_type=F32, precision=lax.Precision.HIGHEST)
    pad = jnp.zeros((chunk - 2 * bsz, chunk), F32)
    cols = jnp.concatenate([bc_r, ig_r, pad], axis=0).T
    causal = (lax.broadcasted_iota(jnp.int32, (chunk, chunk), 0)
              >= lax.broadcasted_iota(jnp.int32, (chunk, chunk), 1))

    for b in range(bsz):
        sl = pl.ds(b, chunk, stride=bsz)
        q = _get_rows(q_s, sl)
        k = _get_rows(k_s, sl)
        v = _get_rows(v_s, sl)
        b_c = cols[:, b:b + 1]
        i_c = cols[:, bsz + b:bsz + b + 1]
        b_r = bc_r[b:b + 1, :]
        i_r = ig_r[b:b + 1, :]
        m0 = m_s[b:b + 1, 0:1]
        c0 = c_s[b]
        n0 = n_s[b:b + 1, :]

        log_intra = jnp.where(causal, b_c - b_r + i_r, NEG)
        log_init = b_c + m0
        m_t = jnp.maximum(log_init, jnp.max(log_intra, axis=1, keepdims=True))
        w_intra = jnp.exp(log_intra - m_t)
        w_init = jnp.exp(log_init - m_t)
        qb = q.astype(BF16)
        s = lax.dot_general(qb, k.astype(BF16), (((1,), (1,)), ((), ())),
                            preferred_element_type=F32) * w_intra
        num = _dot(s, v) + w_init * jnp.dot(qb, c0.astype(BF16), preferred_element_type=F32)
        den = jnp.sum(s, axis=1, keepdims=True) + w_init * jnp.sum(q * n0, axis=1, keepdims=True)
        hc = num / jnp.maximum(jnp.abs(den), jnp.exp(-m_t))

        b_last = b_r[:, chunk - 1:chunk]
        m_new = jnp.maximum(b_last + m0, jnp.max(b_last - b_r + i_r, axis=1, keepdims=True))
        w_state = jnp.exp(b_last - b_c + i_c - m_new)
        decay = jnp.exp(b_last + m0 - m_new)
        kw = k * w_state
        c_s[b] = decay * c0 + lax.dot_general(kw.astype(BF16), v.astype(BF16), (((0,), (0,)), ((), ())),
                                              preferred_element_type=F32)
        n_s[b:b + 1, :] = decay * n0 + jnp.sum(kw, axis=0, keepdims=True)
        m_s[b:b + 1, :] = jnp.broadcast_to(m_new, (1, LANES))

        hm = _sigmoid(_get_rows(og_s, sl)) * hc
        mu = jnp.mean(hm, -1, keepdims=True)
        hd = hm - mu
        var = jnp.mean(hd * hd, -1, keepdims=True)
        hn = hd * lax.rsqrt(var + LN_EPS) * ng_ref[...] + sk_ref[...] * _get_rows(xc_s, sl)
        am = hn * _silu(_get_rows(zm_s, sl))
        for half in range(M_HD // LANES):
            o_s[half, sl, :] = am[:, half * LANES:(half + 1) * LANES]

    am_ref[...] = jnp.concatenate([o_s[half] for half in range(M_HD // LANES)], axis=1)

    @pl.when(ci == pl.num_programs(1) - 1)
    def _():
        c_out[:, 0] = c_s[...]
        n_out[...] = n_s[...]
        m_out[0] = m_s[...]
        conv_out[...] = halo[...]


def _mlstm_prompt(z_main, g_rows, p, bsz, t):
    chunk = min(M_CHUNK, t)
    rows = bsz * chunk
    nchunks = t // chunk
    kern = functools.partial(_mlstm_chunk_kernel, bsz=bsz, chunk=chunk)
    col = lambda off: (lambda h, c: (c, off + h))
    head_w = lambda h, c: (h, 0, 0)
    head_v = lambda h, c: (0, h)
    out_shapes = (
        jax.ShapeDtypeStruct((t * bsz, D_MODEL), F32),
        jax.ShapeDtypeStruct((bsz, M_HEADS, M_HD, M_HD), F32),
        jax.ShapeDtypeStruct((bsz, D_MODEL), F32),
        jax.ShapeDtypeStruct((M_HEADS, bsz, LANES), F32),
        jax.ShapeDtypeStruct(((CONV_W - 1) * bsz, D_MODEL), F32),
    )
    return pl.pallas_call(
        kern,
        grid=(M_HEADS, nchunks),
        in_specs=[
            pl.BlockSpec((rows, M_HD), col(0)),
            pl.BlockSpec((rows, M_HD), col(4)),
            pl.BlockSpec((rows, M_HD), col(8)),
            pl.BlockSpec((1, 2, bsz, chunk), lambda h, c: (h, 0, 0, c)),
            pl.BlockSpec((CONV_W, M_HD), head_v),
            pl.BlockSpec((1, M_HD), head_v),
            pl.BlockSpec((1, M_HD, M_HD), head_w),
            pl.BlockSpec((1, M_HD, M_HD), head_w),
            pl.BlockSpec((1, M_HD, M_HD), head_w),
            pl.BlockSpec((1, bsz, 2), head_w),
            pl.BlockSpec((1, M_HD), head_v),
            pl.BlockSpec((1, M_HD), head_v),
        ],
        out_specs=(
            pl.BlockSpec((rows, M_HD), lambda h, c: (c, h)),
            pl.BlockSpec((bsz, 1, M_HD, M_HD), lambda h, c: (0, h, 0, 0)),
            pl.BlockSpec((bsz, M_HD), head_v),
            pl.BlockSpec((1, bsz, LANES), head_w),
            pl.BlockSpec(((CONV_W - 1) * bsz, M_HD), head_v),
        ),
        out_shape=out_shapes,
        scratch_shapes=[
            pltpu.VMEM(((CONV_W - 1) * bsz, M_HD), F32),
        ] + [pltpu.VMEM((M_HD // LANES, rows, LANES), F32)] * 7 + [
            pltpu.VMEM((bsz, M_HD, M_HD), F32),
            pltpu.VMEM((bsz, M_HD), F32),
            pltpu.VMEM((bsz, LANES), F32),
        ],
        compiler_params=_cparams(("arbitrary", "arbitrary")),
        name="mlstm_prompt",
    )(z_main, z_main, z_main, g_rows, p['m_conv_w'], p['m_conv_b'].reshape(1, -1),
      p['m_wq'].astype(BF16), p['m_wk'].astype(BF16), p['m_wv'].astype(BF16),
      jnp.broadcast_to(jnp.stack([p['m_ig_b'], p['m_fg_b']], -1)[:, None, :], (M_HEADS, bsz, 2)),
      p['m_norm_g'].reshape(1, -1), p['m_skip'].reshape(1, -1))


DEC_GROUP = 8


def _mlstm_step_kernel(xm_ref, og_ref, zm_ref, gd_ref, cs0_ref, cs1_ref, cs2_ref, c0_ref, n0_ref,
                       cw_ref, cb_ref, wq_ref, wk_ref, wv_ref, wqt_ref, wkt_ref, ng_ref, sk_ref,
                       am_ref, c_out, n_out, m_out):
    xm = xm_ref[...]
    acc = cb_ref[...] + cw_ref[CONV_W - 1:CONV_W, :] * xm
    for j, cs_ref in enumerate((cs0_ref, cs1_ref, cs2_ref)):
        acc = acc + cw_ref[j:j + 1, :] * cs_ref[...]
    xc = _silu(acc)

    xcb = xc.astype(BF16)
    scale = M_HD ** -0.5
    nt = (((1,), (1,)), ((), ()))
    q_r = jnp.dot(xcb, wq_ref[0], preferred_element_type=F32) * scale
    k_r = jnp.dot(xcb, wk_ref[0], preferred_element_type=F32)
    v_r = jnp.dot(xm.astype(BF16), wv_ref[0], preferred_element_type=F32)
    q_t = lax.dot_general(wqt_ref[0], xcb, nt, preferred_element_type=F32) * scale
    k_t = lax.dot_general(wkt_ref[0], xcb, nt, preferred_element_type=F32)

    g = gd_ref[0]
    ig = g[:, 0:1]
    lf = _log_sigmoid(g[:, 1:2])
    m0 = g[:, 2:3]
    m_t = jnp.maximum(lf + m0, ig)
    w_in = jnp.exp(ig - m_t)
    w_st = jnp.exp(lf + m0 - m_t)
    n0 = n0_ref[...]
    qk = jnp.sum(q_r * k_r, axis=1, keepdims=True)
    s = qk * w_in
    den = s + w_st * jnp.sum(q_r * n0, axis=1, keepdims=True)
    n_out[...] = w_st * n0 + w_in * k_r
    m_out[0] = jnp.broadcast_to(m_t, (DEC_GROUP, 8))
    inv = 1.0 / jnp.maximum(jnp.abs(den), jnp.exp(-m_t))

    hs = []
    for b in range(DEC_GROUP):
        c0 = c0_ref[b, 0]
        qc = q_t[:, b:b + 1]
        kc = k_t[:, b:b + 1]
        vr = v_r[b:b + 1, :]
        qcm = jnp.sum(qc * c0, axis=0, keepdims=True)
        num = s[b:b + 1, :] * vr + w_st[b:b + 1, :] * qcm
        hs.append(num * inv[b:b + 1, :])
        c_out[b, 0] = w_st[b:b + 1, :] * c0 + (w_in[b:b + 1, :] * kc) * vr
    hc = jnp.concatenate(hs, axis=0)

    hm = _sigmoid(og_ref[...]) * hc
    mu = jnp.mean(hm, -1, keepdims=True)
    hd = hm - mu
    var = jnp.mean(hd * hd, -1, keepdims=True)
    hn = hd * lax.rsqrt(var + LN_EPS) * ng_ref[...] + sk_ref[...] * xc
    am_ref[...] = hn * _silu(zm_ref[...])


def _mlstm_step(z_main, gdec, conv0, c0, n0, p):
    n = z_main.shape[0]
    gsz = DEC_GROUP
    col = lambda off: (lambda i, h: (i, off + h))
    head_w = lambda i, h: (h, 0, 0)
    head_v = lambda i, h: (0, h)
    out_shapes = (
        jax.ShapeDtypeStruct((n, D_MODEL), F32),
        jax.ShapeDtypeStruct((n, M_HEADS, M_HD, M_HD), F32),
        jax.ShapeDtypeStruct((n, D_MODEL), F32),
        jax.ShapeDtypeStruct((M_HEADS, n, 8), F32),
    )
    conv_flat = conv0.reshape(n, (CONV_W - 1) * D_MODEL)
    wq = p['m_wq'].astype(BF16)
    wk = p['m_wk'].astype(BF16)
    return pl.pallas_call(
        _mlstm_step_kernel,
        grid=(n // gsz, M_HEADS),
        in_specs=[
            pl.BlockSpec((gsz, M_HD), col(0)),
            pl.BlockSpec((gsz, M_HD), col(4)),
            pl.BlockSpec((gsz, M_HD), col(8)),
            pl.BlockSpec((1, gsz, 8), lambda i, h: (h, i, 0)),
            pl.BlockSpec((gsz, M_HD), col(0)),
            pl.BlockSpec((gsz, M_HD), col(4)),
            pl.BlockSpec((gsz, M_HD), col(8)),
            pl.BlockSpec((gsz, 1, M_HD, M_HD), lambda i, h: (i, h, 0, 0)),
            pl.BlockSpec((gsz, M_HD), lambda i, h: (i, h)),
            pl.BlockSpec((CONV_W, M_HD), head_v),
            pl.BlockSpec((1, M_HD), head_v),
            pl.BlockSpec((1, M_HD, M_HD), head_w),
            pl.BlockSpec((1, M_HD, M_HD), head_w),
            pl.BlockSpec((1, M_HD, M_HD), head_w),
            pl.BlockSpec((1, M_HD, M_HD), head_w),
            pl.BlockSpec((1, M_HD, M_HD), head_w),
            pl.BlockSpec((1, M_HD), head_v),
            pl.BlockSpec((1, M_HD), head_v),
        ],
        out_specs=(
            pl.BlockSpec((gsz, M_HD), lambda i, h: (i, h)),
            pl.BlockSpec((gsz, 1, M_HD, M_HD), lambda i, h: (i, h, 0, 0)),
            pl.BlockSpec((gsz, M_HD), lambda i, h: (i, h)),
            pl.BlockSpec((1, gsz, 8), lambda i, h: (h, i, 0)),
        ),
        out_shape=out_shapes,
        compiler_params=_cparams(("arbitrary", "arbitrary")),
        name="mlstm_step",
    )(z_main, z_main, z_main, gdec, conv_flat, conv_flat, conv_flat, c0, n0,
      p['m_conv_w'], p['m_conv_b'].reshape(1, -1), wq, wk, p['m_wv'].astype(BF16),
      jnp.swapaxes(wq, 1, 2), jnp.swapaxes(wk, 1, 2),
      p['m_norm_g'].reshape(1, -1), p['m_skip'].reshape(1, -1))


def _rwkv_mix(cur, prev, mu, w2a, w0, a0):
    xr = cur + mu * (prev - cur)
    la = xr[:, 3 * D_MODEL:]
    lane = lax.broadcasted_iota(jnp.int32, la.shape, 1)
    la = jnp.where(lane < R_LORA, jnp.tanh(la), la)
    lo = jnp.dot(la.astype(BF16), w2a, preferred_element_type=F32)
    wlog = _log_sigmoid(w0 + lo[:, :D_MODEL]) - 0.5
    decay = jnp.exp(-jnp.exp(wlog))
    a = _sigmoid(a0 + lo[:, D_MODEL:])
    return xr[:, :D_MODEL], xr[:, D_MODEL:2 * D_MODEL], xr[:, 2 * D_MODEL:3 * D_MODEL], decay, a


def _rwkv_prep_kernel(zr_ref, sh_ref, mu_ref, w2a_ref, w0_ref, a0_ref,
                      r_out, k_out, v_out, d_out, a_out, halo, *, bsz):
    tm = zr_ref.shape[0]

    @pl.when(pl.program_id(0) == 0)
    def _():
        halo[...] = sh_ref[...]

    cur = zr_ref[:, :R_SHIFT_W]
    if tm > bsz:
        prev = jnp.concatenate([halo[...], cur[:tm - bsz]], axis=0)
    else:
        prev = halo[...]
    halo[...] = cur[tm - bsz:]
    r, k, v, d, a = _rwkv_mix(cur, prev, mu_ref[...], w2a_ref[...], w0_ref[...], a0_ref[...])
    r_out[...] = r
    k_out[...] = k
    v_out[...] = v
    d_out[...] = d
    a_out[...] = a


def _lora_weights(p):
    w2a = jnp.zeros((2 * R_LORA, 2 * D_MODEL), F32)
    return w2a.at[:R_LORA, :D_MODEL].set(p['r_w2']).at[R_LORA:, D_MODEL:].set(p['r_a2']).astype(BF16)


def _rwkv_prep(z_r, shift0, p, bsz):
    n = z_r.shape[0]
    tm = min(n, 512)
    w2a = _lora_weights(p)
    row = lambda i: (i, 0)
    fix = lambda i: (0, 0)
    out = jax.ShapeDtypeStruct((n, D_MODEL), F32)
    return pl.pallas_call(
        functools.partial(_rwkv_prep_kernel, bsz=bsz),
        grid=(n // tm,),
        in_specs=[pl.BlockSpec((tm, Z_R_W), row),
                  pl.BlockSpec((bsz, R_SHIFT_W), fix),
                  pl.BlockSpec((1, R_SHIFT_W), fix),
                  pl.BlockSpec((2 * R_LORA, 2 * D_MODEL), fix),
                  pl.BlockSpec((1, D_MODEL), fix),
                  pl.BlockSpec((1, D_MODEL), fix)],
        out_specs=tuple(pl.BlockSpec((tm, D_MODEL), row) for _ in range(5)),
        out_shape=(out,) * 5,
        scratch_shapes=[pltpu.VMEM((bsz, R_SHIFT_W), F32)],
        compiler_params=_cparams(("arbitrary",)),
        name="rwkv_prep",
    )(z_r, shift0, p['r_mu'].reshape(1, -1), w2a, p['r_w0'].reshape(1, -1), p['r_a0'].reshape(1, -1))


def _rwkv_scan_kernel(r_ref, d_ref, k_ref, v_ref, a_ref, kk_ref, ka_ref, rk_ref, g_ref, b_ref, s0_ref,
                      y_ref, s_out, s_s, y_s):
    ci = pl.program_id(1)
    tc = r_ref.shape[0]

    @pl.when(ci == 0)
    def _():
        s_s[...] = s0_ref[...]

    def step(t, carry):
        y_ref[t] = _rwkv_step(r_ref[t], d_ref[t], k_ref[t], v_ref[t], a_ref[t],
                              kk_ref, ka_ref, rk_ref, g_ref, b_ref, s_s, y_s)
        return carry

    lax.fori_loop(0, tc, step, 0)

    @pl.when(ci == pl.num_programs(1) - 1)
    def _():
        s_out[...] = s_s[...]


def _rwkv_step(r, w, kr, v, a, kk_ref, ka_ref, rk_ref, g_ref, b_ref, s_s, y_s):
    kk = kr * kk_ref[...]
    nrm = jnp.sqrt(jnp.sum(kk * kk, axis=0, keepdims=True))
    kk = kk / jnp.maximum(nrm, 1e-12)
    k = kr * (1.0 + (a - 1.0) * ka_ref[...])
    nkk = -kk
    kka = kk * a
    for i in range(R_HD):
        s_i = s_s[i]
        sa = jnp.sum(s_i * nkk, axis=0, keepdims=True)
        s_n = s_i * w + sa * kka + v[i:i + 1, :] * k
        s_s[i] = s_n
        y_s[i:i + 1, :] = jnp.sum(s_n * r, axis=0, keepdims=True)
    y = y_s[...]
    mu = jnp.mean(y, axis=0, keepdims=True)
    yc = y - mu
    var = jnp.mean(yc * yc, axis=0, keepdims=True)
    bonus = jnp.sum(r * k * rk_ref[...], axis=0, keepdims=True) * v
    return yc * lax.rsqrt(var + RWKV_GN_EPS) * g_ref[...] + b_ref[...] + bonus


PAIR = 2


def _to_lane_pair(x2):
    nb = x2.shape[0] // PAIR
    tile = jnp.concatenate([x2[tt * nb:(tt + 1) * nb, hp * LANES:(hp + 1) * LANES]
                            for tt in range(PAIR) for hp in range(D_MODEL // LANES)], axis=0)
    tt_ = tile.T
    top, bot = tt_[:R_HD], tt_[R_HD:]
    low = lax.broadcasted_iota(jnp.int32, (R_HD, LANES), 1) < R_HD
    return (jnp.where(low, top, pltpu.roll(bot, R_HD, axis=1)),
            jnp.where(low, pltpu.roll(top, R_HD, axis=1), bot))


def _from_lane_pair(y0, y1):
    low = lax.broadcasted_iota(jnp.int32, (R_HD, LANES), 1) < R_HD
    top = jnp.where(low, y0, pltpu.roll(y1, R_HD, axis=1))
    bot = jnp.where(low, pltpu.roll(y0, R_HD, axis=1), y1)
    return jnp.concatenate([top, bot], axis=0).T


def _rwkv_prompt_kernel(zr_ref, mu_ref, w2a_ref, w0_ref, a0_ref, kk_ref, ka_ref, rk_ref, g_ref, b_ref,
                        y_ref, s_out, halo, r_c, d_c, k_c, v_c, a_c, y_c, s_s, y_s, *, bsz):
    ci = pl.program_id(0)
    rows = zr_ref.shape[0]
    tc = rows // bsz

    @pl.when(ci == 0)
    def _():
        halo[...] = jnp.zeros_like(halo)
        s_s[...] = jnp.zeros_like(s_s)

    cur = zr_ref[:, :R_SHIFT_W]
    prev = jnp.concatenate([halo[...], cur[:rows - bsz]], axis=0)
    halo[...] = cur[rows - bsz:]
    mixed = _rwkv_mix(cur, prev, mu_ref[...], w2a_ref[...], w0_ref[...], a0_ref[...])
    for x, dst in zip(mixed, (r_c, k_c, v_c, d_c, a_c)):
        for tp in range(tc // PAIR):
            lo, hi = _to_lane_pair(x[tp * PAIR * bsz:(tp + 1) * PAIR * bsz])
            dst[PAIR * tp] = lo
            dst[PAIR * tp + 1] = hi

    def step(t, carry):
        y_c[t] = _rwkv_step(r_c[t], d_c[t], k_c[t], v_c[t], a_c[t],
                            kk_ref, ka_ref, rk_ref, g_ref, b_ref, s_s, y_s)
        return carry

    lax.fori_loop(0, tc, step, 0)

    for tp in range(tc // PAIR):
        tile = _from_lane_pair(y_c[PAIR * tp], y_c[PAIR * tp + 1])
        for tt in range(PAIR):
            for hp in range(D_MODEL // LANES):
                src = (tt * (D_MODEL // LANES) + hp) * bsz
                y_ref[(tp * PAIR + tt) * bsz:(tp * PAIR + tt + 1) * bsz, hp * LANES:(hp + 1) * LANES] = (
                    tile[src:src + bsz])

    @pl.when(ci == pl.num_programs(0) - 1)
    def _():
        s_out[...] = s_s[...]


def _param_lanes_prompt(pv, bsz):
    pt = pv.reshape(R_HEADS // 2, 2, R_HD).transpose(2, 1, 0)
    return jnp.broadcast_to(pt[..., None], (R_HD, 2, R_HEADS // 2, bsz)).reshape(R_HD, LANES)


def _rwkv_prompt(z_r, p, bsz, t):
    assert bsz * R_HEADS == LANES and PAIR * bsz * (D_MODEL // LANES) == LANES
    tc = 32
    rows = tc * bsz
    row = lambda c: (c, 0)
    fix = lambda c: (0, 0)
    par = pl.BlockSpec((R_HD, LANES), fix)
    pk = tuple(_param_lanes_prompt(p[k], bsz) for k in ('r_k_k', 'r_k_a', 'r_r_k', 'r_ln_g', 'r_ln_b'))
    seq = pltpu.VMEM((tc, R_HD, LANES), F32)
    yr, s_new = pl.pallas_call(
        functools.partial(_rwkv_prompt_kernel, bsz=bsz),
        grid=(t // tc,),
        in_specs=[pl.BlockSpec((rows, Z_R_W), row),
                  pl.BlockSpec((1, R_SHIFT_W), fix),
                  pl.BlockSpec((2 * R_LORA, 2 * D_MODEL), fix),
                  pl.BlockSpec((1, D_MODEL), fix),
                  pl.BlockSpec((1, D_MODEL), fix)] + [par] * 5,
        out_specs=(pl.BlockSpec((rows, D_MODEL), row),
                   pl.BlockSpec((R_HD, R_HD, LANES), lambda c: (0, 0, 0))),
        out_shape=(jax.ShapeDtypeStruct((t * bsz, D_MODEL), F32),
                   jax.ShapeDtypeStruct((R_HD, R_HD, LANES), F32)),
        scratch_shapes=[pltpu.VMEM((bsz, R_SHIFT_W), F32)] + [seq] * 6 + [
            pltpu.VMEM((R_HD, R_HD, LANES), F32), pltpu.VMEM((R_HD, LANES), F32)],
        compiler_params=_cparams(("arbitrary",)),
        name="rwkv_prompt",
    )(z_r, p['r_mu'].reshape(1, -1), _lora_weights(p), p['r_w0'].reshape(1, -1), p['r_a0'].reshape(1, -1), *pk)
    wkv = s_new.reshape(R_HD, R_HD, 2, R_HEADS // 2, bsz).transpose(4, 3, 2, 0, 1)
    return yr, wkv.reshape(bsz, R_HEADS, R_HD, R_HD)


def _rwkv_scan(r, d, k, v, a, pk, s0):
    t, _, nl = r.shape
    tc = min(t, 32)
    seq = pl.BlockSpec((tc, R_HD, LANES), lambda l, c: (c, 0, l))
    par = pl.BlockSpec((R_HD, LANES), lambda l, c: (0, l))
    st = pl.BlockSpec((R_HD, R_HD, LANES), lambda l, c: (0, 0, l))
    return pl.pallas_call(
        _rwkv_scan_kernel,
        grid=(nl // LANES, t // tc),
        in_specs=[seq] * 5 + [par] * 5 + [st],
        out_specs=(seq, st),
        out_shape=(jax.ShapeDtypeStruct((t, R_HD, nl), F32),
                   jax.ShapeDtypeStruct((R_HD, R_HD, nl), F32)),
        scratch_shapes=[pltpu.VMEM((R_HD, R_HD, LANES), F32), pltpu.VMEM((R_HD, LANES), F32)],
        compiler_params=_cparams(("arbitrary", "arbitrary")),
        name="rwkv_scan",
    )(r, d, k, v, a, *pk, s0)


def _to_lanes(x, t, bsz):
    return x.reshape(t, bsz, R_HEADS, R_HD).transpose(0, 3, 1, 2).reshape(t, R_HD, bsz * R_HEADS)


def _from_lanes(y, t, bsz):
    return y.reshape(t, R_HD, bsz, R_HEADS).transpose(0, 2, 3, 1).reshape(t * bsz, D_MODEL)


def _param_lanes(pv, bsz):
    pt = pv.reshape(R_HEADS, R_HD).T
    return jnp.broadcast_to(pt[:, None, :], (R_HD, bsz, R_HEADS)).reshape(R_HD, bsz * R_HEADS)


S5_COLS = 1024
S5_CH = S5_COLS // S_STATE * S_GROUP


def _s5_kernel(u_ref, bre_ref, bim_ref, cre_ref, cim_ref, lre_ref, lim_ref, d_ref, s0re_ref, s0im_ref,
               y_ref, sre_out, sim_out, xre, xim, sre, sim, *, bsz):
    ci = pl.program_id(1)
    tc = u_ref.shape[0] // bsz

    @pl.when(ci == 0)
    def _():
        sre[...] = s0re_ref[...]
        sim[...] = s0im_ref[...]

    u = u_ref[...]
    ub = u.astype(BF16)
    xre[...] = jnp.dot(ub, bre_ref[0], preferred_element_type=F32)
    xim[...] = jnp.dot(ub, bim_ref[0], preferred_element_type=F32)
    lre = jnp.broadcast_to(lre_ref[...], (bsz, S5_COLS))
    lim = jnp.broadcast_to(lim_ref[...], (bsz, S5_COLS))

    def step(t, carry):
        s_re, s_im = carry
        rows = pl.ds(pl.multiple_of(t * bsz, bsz), bsz)
        n_re = lre * s_re - lim * s_im + xre[rows, :]
        n_im = lre * s_im + lim * s_re + xim[rows, :]
        xre[rows, :] = n_re
        xim[rows, :] = n_im
        return n_re, n_im

    f_re, f_im = lax.fori_loop(0, tc, step, (sre[...], sim[...]))
    sre[...] = f_re
    sim[...] = f_im
    ys = (jnp.dot(xre[...].astype(BF16), cre_ref[0], preferred_element_type=F32)
          - jnp.dot(xim[...].astype(BF16), cim_ref[0], preferred_element_type=F32))
    y_ref[...] = _gelu_tanh(ys + d_ref[...] * u)

    @pl.when(ci == pl.num_programs(1) - 1)
    def _():
        sre_out[...] = sre[...]
        sim_out[...] = sim[...]


def _s5_weights(p):
    lam_re = jnp.minimum(p['s_lam_re'], -1e-4)
    lam_im = p['s_lam_im']
    dt = jnp.exp(p['s_log_dt'])[:, None]
    mag = jnp.exp(lam_re * dt)
    lb_re = mag * jnp.cos(lam_im * dt)
    lb_im = mag * jnp.sin(lam_im * dt)
    den = lam_re * lam_re + lam_im * lam_im
    nr = lb_re - 1.0
    coef_re = (nr * lam_re + lb_im * lam_im) / den
    coef_im = (lb_im * lam_re - nr * lam_im) / den
    b_re, b_im = p['s_b_re'], p['s_b_im']
    bb_re = coef_re[..., None] * b_re - coef_im[..., None] * b_im
    bb_im = coef_re[..., None] * b_im + coef_im[..., None] * b_re
    nblk = S_GROUPS * S_STATE // S5_COLS
    gpb = S_GROUPS // nblk
    eye = jnp.eye(gpb, dtype=F32)

    def pack_b(bb):
        x = bb.reshape(nblk, gpb, S_STATE, S_GROUP)
        return jnp.einsum('ngpc,gh->ngchp', x, eye).reshape(nblk, gpb * S_GROUP, gpb * S_STATE).astype(BF16)

    def pack_c(cc):
        x = cc.reshape(nblk, gpb, S_GROUP, S_STATE)
        return jnp.einsum('ngcp,gh->ngphc', x, eye).reshape(nblk, gpb * S_STATE, gpb * S_GROUP).astype(BF16)

    return (pack_b(bb_re), pack_b(bb_im), pack_c(p['s_c_re']), pack_c(p['s_c_im']),
            lb_re.reshape(1, -1), lb_im.reshape(1, -1))


def _s5(z_main, s0re, s0im, p, bsz, t):
    n = z_main.shape[0]
    tc = min(t, 128)
    rows = tc * bsz
    nblk = S_GROUPS * S_STATE // S5_COLS
    bre, bim, cre, cim, lre, lim = _s5_weights(p)
    u_col0 = 4 * D_MODEL // S5_CH
    blk = lambda j, c: (j, 0, 0)
    cols = lambda j, c: (0, j)
    return pl.pallas_call(
        functools.partial(_s5_kernel, bsz=bsz),
        grid=(nblk, t // tc),
        in_specs=[pl.BlockSpec((rows, S5_CH), lambda j, c: (c, u_col0 + j)),
                  pl.BlockSpec((1, S5_CH, S5_COLS), blk),
                  pl.BlockSpec((1, S5_CH, S5_COLS), blk),
                  pl.BlockSpec((1, S5_COLS, S5_CH), blk),
                  pl.BlockSpec((1, S5_COLS, S5_CH), blk),
                  pl.BlockSpec((1, S5_COLS), cols),
                  pl.BlockSpec((1, S5_COLS), cols),
                  pl.BlockSpec((1, S5_CH), cols),
                  pl.BlockSpec((bsz, S5_COLS), cols),
                  pl.BlockSpec((bsz, S5_COLS), cols)],
        out_specs=(pl.BlockSpec((rows, S5_CH), lambda j, c: (c, j)),
                   pl.BlockSpec((bsz, S5_COLS), cols),
                   pl.BlockSpec((bsz, S5_COLS), cols)),
        out_shape=(jax.ShapeDtypeStruct((n, D_MODEL), F32),
                   jax.ShapeDtypeStruct((bsz, S_GROUPS * S_STATE), F32),
                   jax.ShapeDtypeStruct((bsz, S_GROUPS * S_STATE), F32)),
        scratch_shapes=[pltpu.VMEM((rows, S5_COLS), F32), pltpu.VMEM((rows, S5_COLS), F32),
                        pltpu.VMEM((bsz, S5_COLS), F32), pltpu.VMEM((bsz, S5_COLS), F32)],
        compiler_params=_cparams(("arbitrary", "arbitrary")),
        name="s5_scan",
    )(z_main, bre, bim, cre, cim, lre, lim, p['s_d'].reshape(1, -1), s0re, s0im)


def _merge_kernel(x_ref, am_ref, yr_ref, ys_ref, zr_ref, zs_ref, g0_ref, g1_ref, g2_ref,
                  wbm_ref, wbr_ref, wbs_ref, wglu_ref, bglu_ref, wout_ref, lg_ref, lb_ref, o_ref):
    y_m = _dot(am_ref[...], wbm_ref[...])
    y_r = _dot(yr_ref[...] * _silu(zr_ref[...]), wbr_ref[...])
    ys = ys_ref[...]
    gl = ys * _sigmoid(_dot(ys, wglu_ref[...]) + bglu_ref[...])
    y_s = _dot(gl * _silu(zs_ref[...]), wbs_ref[...])
    merged = _sigmoid(g0_ref[...]) * y_m + _sigmoid(g1_ref[...]) * y_r + _sigmoid(g2_ref[...]) * y_s
    h = ALPHA * x_ref[...] + _dot(merged, wout_ref[...])
    mu = jnp.mean(h, -1, keepdims=True)
    hc = h - mu
    var = jnp.mean(hc * hc, -1, keepdims=True)
    o_ref[...] = hc * lax.rsqrt(var + LN_EPS) * lg_ref[...] + lb_ref[...]


def _merge(x, am, yr, ys, z_main, p):
    n = x.shape[0]
    tm = min(n, 256)
    row = lambda i: (i, 0)
    zcol = lambda c: (lambda i: (i, c))
    fix = lambda i: (0, 0)
    act = pl.BlockSpec((tm, D_MODEL), row)
    wsp = pl.BlockSpec((D_MODEL, D_MODEL), fix)
    vsp = pl.BlockSpec((1, D_MODEL), fix)
    return pl.pallas_call(
        _merge_kernel,
        grid=(n // tm,),
        in_specs=[act, act, act, act,
                  pl.BlockSpec((tm, D_MODEL), zcol(3)), pl.BlockSpec((tm, D_MODEL), zcol(5)),
                  pl.BlockSpec((tm, D_MODEL), zcol(6)), pl.BlockSpec((tm, D_MODEL), zcol(7)),
                  pl.BlockSpec((tm, D_MODEL), zcol(8)),
                  wsp, wsp, wsp, wsp, vsp, wsp, vsp, vsp],
        out_specs=act,
        out_shape=jax.ShapeDtypeStruct((n, D_MODEL), F32),
        compiler_params=_cparams(("arbitrary",)),
        name="merge_out",
    )(x, am, yr, ys, z_main, z_main, z_main, z_main, z_main,
      p['w_bm'].astype(BF16), p['w_br'].astype(BF16), p['w_bs'].astype(BF16),
      p['s_glu_w'].astype(BF16), p['s_glu_b'].reshape(1, -1), p['w_out'].astype(BF16),
      p['ln_g'].reshape(1, -1), p['ln_b'].reshape(1, -1))


GATE_COLS = 2 * M_HEADS


def _wperm_kernel(a_ref, b_ref, o_ref, *, shifted):
    a = a_ref[0]
    both = jnp.concatenate([a, b_ref[0]], axis=1)
    sh = pltpu.roll(both, 2 * LANES - GATE_COLS, axis=1)[:, :LANES]
    o_ref[...] = jnp.where(shifted(pl.program_id(0)), sh, a).astype(BF16)


def _permute_w(w_all, layer, nblk, src, shifted, name):
    spec = lambda off: pl.BlockSpec((1, D_MODEL, LANES), lambda k: (layer, 0, src(k) + off))
    return pl.pallas_call(
        functools.partial(_wperm_kernel, shifted=shifted),
        grid=(nblk,),
        in_specs=[spec(0), spec(1)],
        out_specs=pl.BlockSpec((D_MODEL, LANES), lambda k: (0, k)),
        out_shape=jax.ShapeDtypeStruct((D_MODEL, nblk * LANES), BF16),
        compiler_params=_cparams(("arbitrary",)),
        name=name,
    )(w_all, w_all)


def _split_w_in(w_all, layer):
    first_r = (3 * D_MODEL) // LANES
    n_main = Z_MAIN_W // LANES
    n_r = R_SHIFT_W // LANES
    w_main = _permute_w(w_all, layer, n_main,
                        lambda k: jnp.where(k >= first_r, k + n_r, k),
                        lambda k: k >= D_MODEL // LANES, "wperm_main")
    w_r = _permute_w(w_all, layer, n_r + 1,
                     lambda k: jnp.where(k == n_r, D_MODEL // LANES, k + first_r),
                     lambda k: k < n_r, "wperm_shift")
    return w_main, w_r


def _layer(x, state, p, w_main, w_r, bsz, t):
    n = t * bsz
    z_main = _matmul(x, w_main, 1536, "proj_main")
    z_r = _matmul(x, w_r, Z_R_W // 2, "proj_shift")
    gates = z_r[:, R_SHIFT_W:R_SHIFT_W + 2 * M_HEADS]

    if state is None:
        chunk = min(M_CHUNK, t)
        g_rows = gates.reshape(t, bsz, 2, M_HEADS).transpose(3, 2, 1, 0)
        am, c_new, n_new, m_new, conv_new = _mlstm_prompt(z_main, g_rows, p, bsz, t)
        m_new = m_new[:, :, 0].T
        conv_new = conv_new.reshape(CONV_W - 1, bsz, D_MODEL).transpose(1, 0, 2)
    else:
        c0, n0, m0, conv0 = state[:4]
        bias = jnp.stack([p['m_ig_b'], p['m_fg_b']], 0)
        gd = gates.reshape(n, 2, M_HEADS) + bias[None]
        gd = jnp.concatenate([gd, m0[:, None, :], jnp.zeros((n, 5, M_HEADS), F32)], axis=1).transpose(2, 0, 1)
        am, c_new, n_new, m_new = _mlstm_step(z_main, gd, conv0, c0, n0.reshape(n, D_MODEL), p)
        m_new = m_new[:, :, 0].T
        conv_new = jnp.concatenate([conv0[:, 1:], z_main[:, None, :D_MODEL]], axis=1)
    n_new = n_new.reshape(bsz, M_HEADS, M_HD)

    if state is None:
        yr, wkv_new = _rwkv_prompt(z_r, p, bsz, t)
    else:
        r, kr, vr, dec, a = _rwkv_prep(z_r, state[5], p, bsz)
        nl = bsz * R_HEADS
        s0 = state[4].transpose(2, 3, 0, 1).reshape(R_HD, R_HD, nl)
        pk = tuple(_param_lanes(p[k], bsz) for k in ('r_k_k', 'r_k_a', 'r_r_k', 'r_ln_g', 'r_ln_b'))
        yr_l, s_new = _rwkv_scan(*(_to_lanes(v, t, bsz) for v in (r, dec, kr, vr, a)), pk, s0)
        yr = _from_lanes(yr_l, t, bsz)
        wkv_new = s_new.reshape(R_HD, R_HD, bsz, R_HEADS).transpose(2, 3, 0, 1)
    shift_new = z_r[n - bsz:, :R_SHIFT_W]

    if state is None:
        s0re = jnp.zeros((bsz, S_GROUPS * S_STATE), F32)
        s0im = s0re
    else:
        s0re = state[6].reshape(bsz, -1)
        s0im = state[7].reshape(bsz, -1)
    ys, sre_new, sim_new = _s5(z_main, s0re, s0im, p, bsz, t)

    x_new = _merge(x, am, yr, ys, z_main, p)
    new_state = (c_new, n_new, m_new, conv_new, wkv_new, shift_new,
                 sre_new.reshape(bsz, S_GROUPS, S_STATE), sim_new.reshape(bsz, S_GROUPS, S_STATE))
    return x_new, new_state


_PARAM_NAMES = ('w_in', 'm_conv_w', 'm_conv_b', 'm_wq', 'm_wk', 'm_wv', 'm_ig_b', 'm_fg_b', 'm_norm_g', 'm_skip',
                'r_mu', 'r_w0', 'r_w2', 'r_a0', 'r_a2', 'r_k_k', 'r_k_a', 'r_r_k', 'r_ln_g', 'r_ln_b',
                's_lam_re', 's_lam_im', 's_log_dt', 's_b_re', 's_b_im', 's_c_re', 's_c_im', 's_d',
                's_glu_w', 's_glu_b', 'w_bm', 'w_br', 'w_bs', 'w_out', 'ln_g', 'ln_b')


def kernel(x_prompt, x_sample, state_mlstm_c, state_mlstm_n, state_mlstm_m, state_mlstm_conv, state_rwkv_wkv, state_rwkv_shift, state_s5_re, state_s5_im, ln_in_g, ln_in_b, w_in, m_conv_w, m_conv_b, m_wq, m_wk, m_wv, m_ig_b, m_fg_b, m_norm_g, m_skip, r_mu, r_w0, r_w2, r_a0, r_a2, r_k_k, r_k_a, r_r_k, r_ln_g, r_ln_b, s_lam_re, s_lam_im, s_log_dt, s_b_re, s_b_im, s_c_re, s_c_im, s_d, s_glu_w, s_glu_b, w_bm, w_br, w_bs, w_out, ln_g, ln_b):
    weights = dict(zip(_PARAM_NAMES, (w_in, m_conv_w, m_conv_b, m_wq, m_wk, m_wv, m_ig_b, m_fg_b, m_norm_g, m_skip,
                                      r_mu, r_w0, r_w2, r_a0, r_a2, r_k_k, r_k_a, r_r_k, r_ln_g, r_ln_b,
                                      s_lam_re, s_lam_im, s_log_dt, s_b_re, s_b_im, s_c_re, s_c_im, s_d,
                                      s_glu_w, s_glu_b, w_bm, w_br, w_bs, w_out, ln_g, ln_b)))
    caches = (state_mlstm_c, state_mlstm_n, state_mlstm_m, state_mlstm_conv,
              state_rwkv_wkv, state_rwkv_shift, state_s5_re, state_s5_im)
    bp, tp, _ = x_prompt.shape
    bs, ts, _ = x_sample.shape
    xp = _layer_norm_rows(x_prompt.transpose(1, 0, 2).reshape(tp * bp, D_MODEL), ln_in_g, ln_in_b)
    xs = _layer_norm_rows(x_sample.transpose(1, 0, 2).reshape(ts * bs, D_MODEL), ln_in_g, ln_in_b)
    new_p = [[] for _ in caches]
    new_s = [[] for _ in caches]
    for l in range(DEPTH):
        p = {k: v[l] for k, v in weights.items() if k != 'w_in'}
        w_main, w_r = _split_w_in(w_in, l)
        xp, sp = _layer(xp, None, p, w_main, w_r, bp, tp)
        xs, ss = _layer(xs, tuple(c[l] for c in caches), p, w_main, w_r, bs, ts)
        for i in range(len(caches)):
            new_p[i].append(sp[i])
            new_s[i].append(ss[i])
    pc = [jnp.stack(a) for a in new_p]
    sc = [jnp.stack(a) for a in new_s]
    yp = xp.reshape(tp, bp, D_MODEL).transpose(1, 0, 2)
    ys = xs.reshape(ts, bs, D_MODEL).transpose(1, 0, 2)
    return (yp, ys, pc[0], sc[0], pc[1], sc[1], pc[2], sc[2], pc[3], sc[3],
            pc[4], sc[4], pc[5], sc[5], pc[6], sc[6], pc[7], sc[7])
```

```python
import functools
import math

import jax
import jax.numpy as jnp
import numpy as np
from jax import lax
from jax.experimental import pallas as pl
from jax.experimental.pallas import tpu as pltpu

D_MODEL = 1024
DEPTH = 4
M_HEADS = 4
M_HD = 256
CONV_W = 4
M_CHUNK = 128
R_HD = 64
R_HEADS = 16
R_LORA = 64
R_SHIFT_W = 3 * D_MODEL + 2 * R_LORA
S_GROUP = 16
S_GROUPS = 64
S_STATE = 64
ALPHA = (2.0 * DEPTH) ** 0.25
LN_EPS = 1e-5
RWKV_GN_EPS = 64e-5
NEG = -1e30

LANES = 128
Z_MAIN_W = 9 * D_MODEL
Z_R_W = R_SHIFT_W + LANES
VMEM_LIMIT = 56 * 1024 * 1024

BF16 = jnp.bfloat16
F32 = jnp.float32


def _cparams(sem):
    return pltpu.CompilerParams(dimension_semantics=sem, vmem_limit_bytes=VMEM_LIMIT)


def _dot(a, b):
    return jnp.dot(a.astype(BF16), b.astype(BF16), preferred_element_type=F32)


def _sigmoid(x):
    return 1.0 / (1.0 + jnp.exp(-x))


def _silu(x):
    return x * _sigmoid(x)


def _log_sigmoid(x):
    return -(jnp.maximum(-x, 0.0) + jnp.log1p(jnp.exp(-jnp.abs(x))))


def _gelu_tanh(x):
    return 0.5 * x * (1.0 + jnp.tanh(math.sqrt(2.0 / math.pi) * (x + 0.044715 * (x * x * x))))


def _ln_kernel(x_ref, g_ref, b_ref, o_ref):
    x = x_ref[...]
    mu = jnp.mean(x, -1, keepdims=True)
    xc = x - mu
    var = jnp.mean(xc * xc, -1, keepdims=True)
    o_ref[...] = xc * lax.rsqrt(var + LN_EPS) * g_ref[...] + b_ref[...]


def _layer_norm_rows(x, g, b):
    n = x.shape[0]
    tm = min(n, 1024)
    return pl.pallas_call(
        _ln_kernel,
        grid=(n // tm,),
        in_specs=[pl.BlockSpec((tm, D_MODEL), lambda i: (i, 0)),
                  pl.BlockSpec((1, D_MODEL), lambda i: (0, 0)),
                  pl.BlockSpec((1, D_MODEL), lambda i: (0, 0))],
        out_specs=pl.BlockSpec((tm, D_MODEL), lambda i: (i, 0)),
        out_shape=jax.ShapeDtypeStruct((n, D_MODEL), F32),
        compiler_params=_cparams(("arbitrary",)),
        name="ln_in",
    )(x, g.reshape(1, -1), b.reshape(1, -1))


def _mm_kernel(x_ref, w_ref, o_ref):
    o_ref[...] = jnp.dot(x_ref[...].astype(BF16), w_ref[...], preferred_element_type=F32)


def _matmul(x, w, tn, name):
    n, k = x.shape
    nw = w.shape[1]
    tm = min(n, 1024)
    return pl.pallas_call(
        _mm_kernel,
        grid=(n // tm, nw // tn),
        in_specs=[pl.BlockSpec((tm, k), lambda i, j: (i, 0)),
                  pl.BlockSpec((k, tn), lambda i, j: (0, j))],
        out_specs=pl.BlockSpec((tm, tn), lambda i, j: (i, j)),
        out_shape=jax.ShapeDtypeStruct((n, nw), F32),
        compiler_params=_cparams(("arbitrary", "arbitrary")),
        name=name,
    )(x, w)


def _put_halves(dst, x):
    for half in range(M_HD // LANES):
        dst[half] = x[:, half * LANES:(half + 1) * LANES]


def _get_rows(src, sl):
    return jnp.concatenate([src[half, sl, :] for half in range(M_HD // LANES)], axis=1)


def _mlstm_chunk_kernel(xm_ref, og_ref, zm_ref, gr_ref, cw_ref, cb_ref, wq_ref, wk_ref, wv_ref,
                        gb_ref, ng_ref, sk_ref,
                        am_ref, c_out, n_out, m_out, conv_out,
                        halo, xc_s, q_s, k_s, v_s, og_s, zm_s, o_s, c_s, n_s, m_s, *, bsz, chunk):
    ci = pl.program_id(1)
    rows = bsz * chunk
    nhalo = (CONV_W - 1) * bsz

    @pl.when(ci == 0)
    def _():
        halo[...] = jnp.zeros_like(halo)
        c_s[...] = jnp.zeros_like(c_s)
        n_s[...] = jnp.zeros_like(n_s)
        m_s[...] = jnp.full_like(m_s, NEG)

    xm = xm_ref[...]
    xp = jnp.concatenate([halo[...], xm], axis=0)
    acc = cb_ref[...] + cw_ref[CONV_W - 1:CONV_W, :] * xm
    for j in range(CONV_W - 1):
        acc = acc + cw_ref[j:j + 1, :] * xp[j * bsz:j * bsz + rows]
    xc = _silu(acc)
    halo[...] = xm[rows - nhalo:]
    xcb = xc.astype(BF16)
    _put_halves(xc_s, xc)
    _put_halves(q_s, jnp.dot(xcb, wq_ref[0], preferred_element_type=F32) * (M_HD ** -0.5))
    _put_halves(k_s, jnp.dot(xcb, wk_ref[0], preferred_element_type=F32))
    _put_halves(v_s, jnp.dot(xm.astype(BF16), wv_ref[0], preferred_element_type=F32))
    _put_halves(og_s, og_ref[...])
    _put_halves(zm_s, zm_ref[...])

    ig_r = gr_ref[0, 0] + gb_ref[0, :, 0:1]
    lf_r = _log_sigmoid(gr_ref[0, 1] + gb_ref[0, :, 1:2])
    tri = (lax.broadcasted_iota(jnp.int32, (chunk, chunk), 0)
           <= lax.broadcasted_iota(jnp.int32, (chunk, chunk), 1)).astype(F32)
    bc_r = jnp.dot(lf_r, tri, preferred_element_type=F32, precision=lax.Precision.HIGHEST)
    pad = jnp.zeros((chunk - 2 * bsz, chunk), F32)
    cols = jnp.concatenate([bc_r, ig_r, pad], axis=0).T
    causal = (lax.broadcasted_iota(jnp.int32, (chunk, chunk), 0)
              >= lax.broadcasted_iota(jnp.int32, (chunk, chunk), 1))

    for b in range(bsz):
        sl = pl.ds(b, chunk, stride=bsz)
        q = _get_rows(q_s, sl)
        k = _get_rows(k_s, sl)
        v = _get_rows(v_s, sl)
        b_c = cols[:, b:b + 1]
        i_c = cols[:, bsz + b:bsz + b + 1]
        b_r = bc_r[b:b + 1, :]
        i_r = ig_r[b:b + 1, :]
        m0 = m_s[b:b + 1, 0:1]
        c0 = c_s[b]
        n0 = n_s[b:b + 1, :]

        log_intra = jnp.where(causal, b_c - b_r + i_r, NEG)
        log_init = b_c + m0
        m_t = jnp.maximum(log_init, jnp.max(log_intra, axis=1, keepdims=True))
        w_intra = jnp.exp(log_intra - m_t)
        w_init = jnp.exp(log_init - m_t)
        qb = q.astype(BF16)
        s = lax.dot_general(qb, k.astype(BF16), (((1,), (1,)), ((), ())),
                            preferred_element_type=F32) * w_intra
        num = _dot(s, v) + w_init * jnp.dot(qb, c0.astype(BF16), preferred_element_type=F32)
        den = jnp.sum(s, axis=1, keepdims=True) + w_init * jnp.sum(q * n0, axis=1, keepdims=True)
        hc = num / jnp.maximum(jnp.abs(den), jnp.exp(-m_t))

        b_last = b_r[:, chunk - 1:chunk]
        m_new = jnp.maximum(b_last + m0, jnp.max(b_last - b_r + i_r, axis=1, keepdims=True))
        w_state = jnp.exp(b_last - b_c + i_c - m_new)
        decay = jnp.exp(b_last + m0 - m_new)
        kw = k * w_state
        c_s[b] = decay * c0 + lax.dot_general(kw.astype(BF16), v.astype(BF16), (((0,), (0,)), ((), ())),
                                              preferred_element_type=F32)
        n_s[b:b + 1, :] = decay * n0 + jnp.sum(kw, axis=0, keepdims=True)
        m_s[b:b + 1, :] = jnp.broadcast_to(m_new, (1, LANES))

        hm = _sigmoid(_get_rows(og_s, sl)) * hc
        mu = jnp.mean(hm, -1, keepdims=True)
        hd = hm - mu
        var = jnp.mean(hd * hd, -1, keepdims=True)
        hn = hd * lax.rsqrt(var + LN_EPS) * ng_ref[...] + sk_ref[...] * _get_rows(xc_s, sl)
        am = hn * _silu(_get_rows(zm_s, sl))
        for half in range(M_HD // LANES):
            o_s[half, sl, :] = am[:, half * LANES:(half + 1) * LANES]

    am_ref[...] = jnp.concatenate([o_s[half] for half in range(M_HD // LANES)], axis=1)

    @pl.when(ci == pl.num_programs(1) - 1)
    def _():
        c_out[:, 0] = c_s[...]
        n_out[...] = n_s[...]
        m_out[0] = m_s[...]
        conv_out[...] = halo[...]


def _mlstm_prompt(z_main, g_rows, p, bsz, t):
    chunk = min(M_CHUNK, t)
    rows = bsz * chunk
    nchunks = t // chunk
    kern = functools.partial(_mlstm_chunk_kernel, bsz=bsz, chunk=chunk)
    col = lambda off: (lambda h, c: (c, off + h))
    head_w = lambda h, c: (h, 0, 0)
    head_v = lambda h, c: (0, h)
    out_shapes = (
        jax.ShapeDtypeStruct((t * bsz, D_MODEL), F32),
        jax.ShapeDtypeStruct((bsz, M_HEADS, M_HD, M_HD), F32),
        jax.ShapeDtypeStruct((bsz, D_MODEL), F32),
        jax.ShapeDtypeStruct((M_HEADS, bsz, LANES), F32),
        jax.ShapeDtypeStruct(((CONV_W - 1) * bsz, D_MODEL), F32),
    )
    return pl.pallas_call(
        kern,
        grid=(M_HEADS, nchunks),
        in_specs=[
            pl.BlockSpec((rows, M_HD), col(0)),
            pl.BlockSpec((rows, M_HD), col(4)),
            pl.BlockSpec((rows, M_HD), col(8)),
            pl.BlockSpec((1, 2, bsz, chunk), lambda h, c: (h, 0, 0, c)),
            pl.BlockSpec((CONV_W, M_HD), head_v),
            pl.BlockSpec((1, M_HD), head_v),
            pl.BlockSpec((1, M_HD, M_HD), head_w),
            pl.BlockSpec((1, M_HD, M_HD), head_w),
            pl.BlockSpec((1, M_HD, M_HD), head_w),
            pl.BlockSpec((1, bsz, 2), head_w),
            pl.BlockSpec((1, M_HD), head_v),
            pl.BlockSpec((1, M_HD), head_v),
        ],
        out_specs=(
            pl.BlockSpec((rows, M_HD), lambda h, c: (c, h)),
            pl.BlockSpec((bsz, 1, M_HD, M_HD), lambda h, c: (0, h, 0, 0)),
            pl.BlockSpec((bsz, M_HD), head_v),
            pl.BlockSpec((1, bsz, LANES), head_w),
            pl.BlockSpec(((CONV_W - 1) * bsz, M_HD), head_v),
        ),
        out_shape=out_shapes,
        scratch_shapes=[
            pltpu.VMEM(((CONV_W - 1) * bsz, M_HD), F32),
        ] + [pltpu.VMEM((M_HD // LANES, rows, LANES), F32)] * 7 + [
            pltpu.VMEM((bsz, M_HD, M_HD), F32),
            pltpu.VMEM((bsz, M_HD), F32),
            pltpu.VMEM((bsz, LANES), F32),
        ],
        compiler_params=_cparams(("arbitrary", "arbitrary")),
        name="mlstm_prompt",
    )(z_main, z_main, z_main, g_rows, p['m_conv_w'], p['m_conv_b'].reshape(1, -1),
      p['m_wq'].astype(BF16), p['m_wk'].astype(BF16), p['m_wv'].astype(BF16),
      jnp.broadcast_to(jnp.stack([p['m_ig_b'], p['m_fg_b']], -1)[:, None, :], (M_HEADS, bsz, 2)),
      p['m_norm_g'].reshape(1, -1), p['m_skip'].reshape(1, -1))


DEC_GROUP = 8


def _mlstm_step_kernel(xm_ref, og_ref, zm_ref, gd_ref, cs0_ref, cs1_ref, cs2_ref, c0_ref, n0_ref,
                       cw_ref, cb_ref, wq_ref, wk_ref, wv_ref, wqt_ref, wkt_ref, ng_ref, sk_ref, *rest):
    am_ref, c_out, n_out, m_out = rest[-4:]
    xm = xm_ref[...]
    acc = cb_ref[...] + cw_ref[CONV_W - 1:CONV_W, :] * xm
    for j, cs_ref in enumerate((cs0_ref, cs1_ref, cs2_ref)):
        acc = acc + cw_ref[j:j + 1, :] * cs_ref[...]
    xc = _silu(acc)

    xcb = xc.astype(BF16)
    scale = M_HD ** -0.5
    nt = (((1,), (1,)), ((), ()))
    q_r = jnp.dot(xcb, wq_ref[0], preferred_element_type=F32) * scale
    k_r = jnp.dot(xcb, wk_ref[0], preferred_element_type=F32)
    v_r = jnp.dot(xm.astype(BF16), wv_ref[0], preferred_element_type=F32)
    q_t = lax.dot_general(wqt_ref[0], xcb, nt, preferred_element_type=F32) * scale
    k_t = lax.dot_general(wkt_ref[0], xcb, nt, preferred_element_type=F32)

    g = gd_ref[0]
    ig = g[:, 0:1]
    lf = _log_sigmoid(g[:, 1:2])
    m0 = g[:, 2:3]
    m_t = jnp.maximum(lf + m0, ig)
    w_in = jnp.exp(ig - m_t)
    w_st = jnp.exp(lf + m0 - m_t)
    n0 = n0_ref[...]
    qk = jnp.sum(q_r * k_r, axis=1, keepdims=True)
    s = qk * w_in
    den = s + w_st * jnp.sum(q_r * n0, axis=1, keepdims=True)
    n_out[...] = w_st * n0 + w_in * k_r
    m_out[0] = jnp.broadcast_to(m_t, (DEC_GROUP, 8))
    inv = 1.0 / jnp.maximum(jnp.abs(den), jnp.exp(-m_t))

    hs = []
    for b in range(DEC_GROUP):
        c0 = c0_ref[0, b, 0]
        qc = q_t[:, b:b + 1]
        kc = k_t[:, b:b + 1]
        vr = v_r[b:b + 1, :]
        qcm = jnp.sum(qc * c0, axis=0, keepdims=True)
        num = s[b:b + 1, :] * vr + w_st[b:b + 1, :] * qcm
        hs.append(num * inv[b:b + 1, :])
        c_out[0, b, 0] = w_st[b:b + 1, :] * c0 + (w_in[b:b + 1, :] * kc) * vr
    hc = jnp.concatenate(hs, axis=0)

    hm = _sigmoid(og_ref[...]) * hc
    mu = jnp.mean(hm, -1, keepdims=True)
    hd = hm - mu
    var = jnp.mean(hd * hd, -1, keepdims=True)
    hn = hd * lax.rsqrt(var + LN_EPS) * ng_ref[...] + sk_ref[...] * xc
    am_ref[...] = hn * _silu(zm_ref[...])


def _mlstm_step(z_main, gdec, conv0, c_all, layer, c_acc, n0, p):
    n = z_main.shape[0]
    gsz = DEC_GROUP
    col = lambda off: (lambda i, h: (i, off + h))
    head_w = lambda i, h: (h, 0, 0)
    head_v = lambda i, h: (0, h)
    c_blk = pl.BlockSpec((1, gsz, 1, M_HD, M_HD), lambda i, h: (layer, i, h, 0, 0))
    out_shapes = (
        jax.ShapeDtypeStruct((n, D_MODEL), F32),
        jax.ShapeDtypeStruct(c_all.shape, F32),
        jax.ShapeDtypeStruct((n, D_MODEL), F32),
        jax.ShapeDtypeStruct((M_HEADS, n, 8), F32),
    )
    conv_flat = conv0.reshape(n, (CONV_W - 1) * D_MODEL)
    wq = p['m_wq'].astype(BF16)
    wk = p['m_wk'].astype(BF16)
    n_fixed = 18
    acc_spec = [] if c_acc is None else [pl.BlockSpec(memory_space=pl.ANY)]
    acc_arg = [] if c_acc is None else [c_acc]
    return pl.pallas_call(
        _mlstm_step_kernel,
        grid=(n // gsz, M_HEADS),
        input_output_aliases={} if c_acc is None else {n_fixed: 1},
        in_specs=[
            pl.BlockSpec((gsz, M_HD), col(0)),
            pl.BlockSpec((gsz, M_HD), col(4)),
            pl.BlockSpec((gsz, M_HD), col(8)),
            pl.BlockSpec((1, gsz, 8), lambda i, h: (h, i, 0)),
            pl.BlockSpec((gsz, M_HD), col(0)),
            pl.BlockSpec((gsz, M_HD), col(4)),
            pl.BlockSpec((gsz, M_HD), col(8)),
            c_blk,
            pl.BlockSpec((gsz, M_HD), lambda i, h: (i, h)),
            pl.BlockSpec((CONV_W, M_HD), head_v),
            pl.BlockSpec((1, M_HD), head_v),
            pl.BlockSpec((1, M_HD, M_HD), head_w),
            pl.BlockSpec((1, M_HD, M_HD), head_w),
            pl.BlockSpec((1, M_HD, M_HD), head_w),
            pl.BlockSpec((1, M_HD, M_HD), head_w),
            pl.BlockSpec((1, M_HD, M_HD), head_w),
            pl.BlockSpec((1, M_HD), head_v),
            pl.BlockSpec((1, M_HD), head_v),
        ] + acc_spec,
        out_specs=(
            pl.BlockSpec((gsz, M_HD), lambda i, h: (i, h)),
            c_blk,
            pl.BlockSpec((gsz, M_HD), lambda i, h: (i, h)),
            pl.BlockSpec((1, gsz, 8), lambda i, h: (h, i, 0)),
        ),
        out_shape=out_shapes,
        compiler_params=_cparams(("arbitrary", "arbitrary")),
        name="mlstm_step",
    )(z_main, z_main, z_main, gdec, conv_flat, conv_flat, conv_flat, c_all, n0,
      p['m_conv_w'], p['m_conv_b'].reshape(1, -1), wq, wk, p['m_wv'].astype(BF16),
      jnp.swapaxes(wq, 1, 2), jnp.swapaxes(wk, 1, 2),
      p['m_norm_g'].reshape(1, -1), p['m_skip'].reshape(1, -1), *acc_arg)


def _rwkv_mix(cur, prev, mu, w2a, w0, a0):
    xr = cur + mu * (prev - cur)
    la = xr[:, 3 * D_MODEL:]
    lane = lax.broadcasted_iota(jnp.int32, la.shape, 1)
    la = jnp.where(lane < R_LORA, jnp.tanh(la), la)
    lo = jnp.dot(la.astype(BF16), w2a, preferred_element_type=F32)
    wlog = _log_sigmoid(w0 + lo[:, :D_MODEL]) - 0.5
    decay = jnp.exp(-jnp.exp(wlog))
    a = _sigmoid(a0 + lo[:, D_MODEL:])
    return xr[:, :D_MODEL], xr[:, D_MODEL:2 * D_MODEL], xr[:, 2 * D_MODEL:3 * D_MODEL], decay, a


def _rwkv_prep_kernel(zr_ref, sh_ref, mu_ref, w2a_ref, w0_ref, a0_ref,
                      r_out, k_out, v_out, d_out, a_out, halo, *, bsz):
    tm = zr_ref.shape[0]

    @pl.when(pl.program_id(0) == 0)
    def _():
        halo[...] = sh_ref[...]

    cur = zr_ref[:, :R_SHIFT_W]
    if tm > bsz:
        prev = jnp.concatenate([halo[...], cur[:tm - bsz]], axis=0)
    else:
        prev = halo[...]
    halo[...] = cur[tm - bsz:]
    r, k, v, d, a = _rwkv_mix(cur, prev, mu_ref[...], w2a_ref[...], w0_ref[...], a0_ref[...])
    r_out[...] = r
    k_out[...] = k
    v_out[...] = v
    d_out[...] = d
    a_out[...] = a


def _lora_weights(p):
    w2a = jnp.zeros((2 * R_LORA, 2 * D_MODEL), F32)
    return w2a.at[:R_LORA, :D_MODEL].set(p['r_w2']).at[R_LORA:, D_MODEL:].set(p['r_a2']).astype(BF16)


def _rwkv_prep(z_r, shift0, p, bsz):
    n = z_r.shape[0]
    tm = min(n, 512)
    w2a = _lora_weights(p)
    row = lambda i: (i, 0)
    fix = lambda i: (0, 0)
    out = jax.ShapeDtypeStruct((n, D_MODEL), F32)
    return pl.pallas_call(
        functools.partial(_rwkv_prep_kernel, bsz=bsz),
        grid=(n // tm,),
        in_specs=[pl.BlockSpec((tm, Z_R_W), row),
                  pl.BlockSpec((bsz, R_SHIFT_W), fix),
                  pl.BlockSpec((1, R_SHIFT_W), fix),
                  pl.BlockSpec((2 * R_LORA, 2 * D_MODEL), fix),
                  pl.BlockSpec((1, D_MODEL), fix),
                  pl.BlockSpec((1, D_MODEL), fix)],
        out_specs=tuple(pl.BlockSpec((tm, D_MODEL), row) for _ in range(5)),
        out_shape=(out,) * 5,
        scratch_shapes=[pltpu.VMEM((bsz, R_SHIFT_W), F32)],
        compiler_params=_cparams(("arbitrary",)),
        name="rwkv_prep",
    )(z_r, shift0, p['r_mu'].reshape(1, -1), w2a, p['r_w0'].reshape(1, -1), p['r_a0'].reshape(1, -1))


def _rwkv_scan_kernel(r_ref, d_ref, k_ref, v_ref, a_ref, kk_ref, ka_ref, rk_ref, g_ref, b_ref, s0_ref,
                      y_ref, s_out, s_s, y_s):
    ci = pl.program_id(1)
    tc = r_ref.shape[0]

    @pl.when(ci == 0)
    def _():
        s_s[...] = s0_ref[...]

    def step(t, carry):
        y_ref[t] = _rwkv_step(r_ref[t], d_ref[t], k_ref[t], v_ref[t], a_ref[t],
                              kk_ref, ka_ref, rk_ref, g_ref, b_ref, s_s, y_s)
        return carry

    lax.fori_loop(0, tc, step, 0)

    @pl.when(ci == pl.num_programs(1) - 1)
    def _():
        s_out[...] = s_s[...]


def _rwkv_step(r, w, kr, v, a, kk_ref, ka_ref, rk_ref, g_ref, b_ref, s_s, y_s):
    kk = kr * kk_ref[...]
    nrm = jnp.sqrt(jnp.sum(kk * kk, axis=0, keepdims=True))
    kk = kk / jnp.maximum(nrm, 1e-12)
    k = kr * (1.0 + (a - 1.0) * ka_ref[...])
    y_s[0] = -kk
    y_s[1] = w
    y_s[2] = kk * a
    y_s[3] = k
    y_s[4] = r
    row = lambda q, j: jnp.broadcast_to(y_s[q, j:j + 1, :], (R_HD, LANES))
    nch = 4
    sa_p = [s_s[c] * row(0, c) for c in range(nch)]
    for j in range(nch, R_HD):
        sa_p[j % nch] = sa_p[j % nch] + s_s[j] * row(0, j)
    sa = (sa_p[0] + sa_p[1]) + (sa_p[2] + sa_p[3])
    y_p = [None, None]
    for j in range(R_HD):
        s_n = s_s[j] * row(1, j) + sa * row(2, j) + v * row(3, j)
        s_s[j] = s_n
        y_p[j % 2] = s_n * row(4, j) if j < 2 else y_p[j % 2] + s_n * row(4, j)
    y = y_p[0] + y_p[1]
    mu = jnp.mean(y, axis=0, keepdims=True)
    yc = y - mu
    var = jnp.mean(yc * yc, axis=0, keepdims=True)
    bonus = jnp.sum(r * k * rk_ref[...], axis=0, keepdims=True) * v
    return yc * lax.rsqrt(var + RWKV_GN_EPS) * g_ref[...] + b_ref[...] + bonus


PAIR = 2


def _to_lane_pair(x2):
    nb = x2.shape[0] // PAIR
    tile = jnp.concatenate([x2[tt * nb:(tt + 1) * nb, hp * LANES:(hp + 1) * LANES]
                            for tt in range(PAIR) for hp in range(D_MODEL // LANES)], axis=0)
    tt_ = tile.T
    top, bot = tt_[:R_HD], tt_[R_HD:]
    low = lax.broadcasted_iota(jnp.int32, (R_HD, LANES), 1) < R_HD
    return (jnp.where(low, top, pltpu.roll(bot, R_HD, axis=1)),
            jnp.where(low, pltpu.roll(top, R_HD, axis=1), bot))


def _from_lane_pair(y0, y1):
    low = lax.broadcasted_iota(jnp.int32, (R_HD, LANES), 1) < R_HD
    top = jnp.where(low, y0, pltpu.roll(y1, R_HD, axis=1))
    bot = jnp.where(low, pltpu.roll(y0, R_HD, axis=1), y1)
    return jnp.concatenate([top, bot], axis=0).T


def _rwkv_prompt_kernel(zr_ref, mu_ref, w2a_ref, w0_ref, a0_ref, kk_ref, ka_ref, rk_ref, g_ref, b_ref,
                        y_ref, s_out, halo, r_c, d_c, k_c, v_c, a_c, y_c, s_s, y_s, *, bsz):
    ci = pl.program_id(0)
    rows = zr_ref.shape[0]
    tc = rows // bsz

    @pl.when(ci == 0)
    def _():
        halo[...] = jnp.zeros_like(halo)
        s_s[...] = jnp.zeros_like(s_s)

    cur = zr_ref[:, :R_SHIFT_W]
    prev = jnp.concatenate([halo[...], cur[:rows - bsz]], axis=0)
    halo[...] = cur[rows - bsz:]
    mixed = _rwkv_mix(cur, prev, mu_ref[...], w2a_ref[...], w0_ref[...], a0_ref[...])
    for x, dst in zip(mixed, (r_c, k_c, v_c, d_c, a_c)):
        for tp in range(tc // PAIR):
            lo, hi = _to_lane_pair(x[tp * PAIR * bsz:(tp + 1) * PAIR * bsz])
            dst[PAIR * tp] = lo
            dst[PAIR * tp + 1] = hi

    def step(t, carry):
        y_c[t] = _rwkv_step(r_c[t], d_c[t], k_c[t], v_c[t], a_c[t],
                            kk_ref, ka_ref, rk_ref, g_ref, b_ref, s_s, y_s)
        return carry

    lax.fori_loop(0, tc, step, 0)

    for tp in range(tc // PAIR):
        tile = _from_lane_pair(y_c[PAIR * tp], y_c[PAIR * tp + 1])
        for tt in range(PAIR):
            for hp in range(D_MODEL // LANES):
                src = (tt * (D_MODEL // LANES) + hp) * bsz
                y_ref[(tp * PAIR + tt) * bsz:(tp * PAIR + tt + 1) * bsz, hp * LANES:(hp + 1) * LANES] = (
                    tile[src:src + bsz])

    @pl.when(ci == pl.num_programs(0) - 1)
    def _():
        s_out[...] = s_s[...]


def _param_lanes_prompt(pv, bsz):
    pt = pv.reshape(R_HEADS // 2, 2, R_HD).transpose(2, 1, 0)
    return jnp.broadcast_to(pt[..., None], (R_HD, 2, R_HEADS // 2, bsz)).reshape(R_HD, LANES)


def _rwkv_prompt(z_r, p, bsz, t):
    assert bsz * R_HEADS == LANES and PAIR * bsz * (D_MODEL // LANES) == LANES
    tc = 32
    rows = tc * bsz
    row = lambda c: (c, 0)
    fix = lambda c: (0, 0)
    par = pl.BlockSpec((R_HD, LANES), fix)
    pk = tuple(_param_lanes_prompt(p[k], bsz) for k in ('r_k_k', 'r_k_a', 'r_r_k', 'r_ln_g', 'r_ln_b'))
    seq = pltpu.VMEM((tc, R_HD, LANES), F32)
    yr, s_new = pl.pallas_call(
        functools.partial(_rwkv_prompt_kernel, bsz=bsz),
        grid=(t // tc,),
        in_specs=[pl.BlockSpec((rows, Z_R_W), row),
                  pl.BlockSpec((1, R_SHIFT_W), fix),
                  pl.BlockSpec((2 * R_LORA, 2 * D_MODEL), fix),
                  pl.BlockSpec((1, D_MODEL), fix),
                  pl.BlockSpec((1, D_MODEL), fix)] + [par] * 5,
        out_specs=(pl.BlockSpec((rows, D_MODEL), row),
                   pl.BlockSpec((R_HD, R_HD, LANES), lambda c: (0, 0, 0))),
        out_shape=(jax.ShapeDtypeStruct((t * bsz, D_MODEL), F32),
                   jax.ShapeDtypeStruct((R_HD, R_HD, LANES), F32)),
        scratch_shapes=[pltpu.VMEM((bsz, R_SHIFT_W), F32)] + [seq] * 6 + [
            pltpu.VMEM((R_HD, R_HD, LANES), F32), pltpu.VMEM((5, R_HD, LANES), F32)],
        compiler_params=_cparams(("arbitrary",)),
        name="rwkv_prompt",
    )(z_r, p['r_mu'].reshape(1, -1), _lora_weights(p), p['r_w0'].reshape(1, -1), p['r_a0'].reshape(1, -1), *pk)
    wkv = s_new.reshape(R_HD, R_HD, 2, R_HEADS // 2, bsz).transpose(4, 3, 2, 1, 0)
    return yr, wkv.reshape(bsz, R_HEADS, R_HD, R_HD)


def _rwkv_scan(r, d, k, v, a, pk, s0):
    t, _, nl = r.shape
    tc = min(t, 32)
    seq = pl.BlockSpec((tc, R_HD, LANES), lambda l, c: (c, 0, l))
    par = pl.BlockSpec((R_HD, LANES), lambda l, c: (0, l))
    st = pl.BlockSpec((R_HD, R_HD, LANES), lambda l, c: (0, 0, l))
    return pl.pallas_call(
        _rwkv_scan_kernel,
        grid=(nl // LANES, t // tc),
        in_specs=[seq] * 5 + [par] * 5 + [st],
        out_specs=(seq, st),
        out_shape=(jax.ShapeDtypeStruct((t, R_HD, nl), F32),
                   jax.ShapeDtypeStruct((R_HD, R_HD, nl), F32)),
        scratch_shapes=[pltpu.VMEM((R_HD, R_HD, LANES), F32), pltpu.VMEM((5, R_HD, LANES), F32)],
        compiler_params=_cparams(("arbitrary", "arbitrary")),
        name="rwkv_scan",
    )(r, d, k, v, a, *pk, s0)


def _to_lanes(x, t, bsz):
    return x.reshape(t, bsz, R_HEADS, R_HD).transpose(0, 3, 1, 2).reshape(t, R_HD, bsz * R_HEADS)


def _from_lanes(y, t, bsz):
    return y.reshape(t, R_HD, bsz, R_HEADS).transpose(0, 2, 3, 1).reshape(t * bsz, D_MODEL)


def _param_lanes(pv, bsz):
    pt = pv.reshape(R_HEADS, R_HD).T
    return jnp.broadcast_to(pt[:, None, :], (R_HD, bsz, R_HEADS)).reshape(R_HD, bsz * R_HEADS)


S5_COLS = 1024
S5_CH = S5_COLS // S_STATE * S_GROUP


def _s5_kernel(u_ref, bre_ref, bim_ref, cre_ref, cim_ref, lre_ref, lim_ref, d_ref, s0re_ref, s0im_ref,
               y_ref, sre_out, sim_out, xre, xim, sre, sim, *, bsz):
    ci = pl.program_id(1)
    tc = u_ref.shape[0] // bsz

    @pl.when(ci == 0)
    def _():
        sre[...] = s0re_ref[...]
        sim[...] = s0im_ref[...]

    u = u_ref[...]
    ub = u.astype(BF16)
    xre[...] = jnp.dot(ub, bre_ref[0], preferred_element_type=F32)
    xim[...] = jnp.dot(ub, bim_ref[0], preferred_element_type=F32)
    lre = jnp.broadcast_to(lre_ref[...], (bsz, S5_COLS))
    lim = jnp.broadcast_to(lim_ref[...], (bsz, S5_COLS))

    def step(t, carry):
        s_re, s_im = carry
        rows = pl.ds(pl.multiple_of(t * bsz, bsz), bsz)
        n_re = lre * s_re - lim * s_im + xre[rows, :]
        n_im = lre * s_im + lim * s_re + xim[rows, :]
        xre[rows, :] = n_re
        xim[rows, :] = n_im
        return n_re, n_im

    f_re, f_im = lax.fori_loop(0, tc, step, (sre[...], sim[...]))
    sre[...] = f_re
    sim[...] = f_im
    ys = (jnp.dot(xre[...].astype(BF16), cre_ref[0], preferred_element_type=F32)
          - jnp.dot(xim[...].astype(BF16), cim_ref[0], preferred_element_type=F32))
    y_ref[...] = _gelu_tanh(ys + d_ref[...] * u)

    @pl.when(ci == pl.num_programs(1) - 1)
    def _():
        sre_out[...] = sre[...]
        sim_out[...] = sim[...]


def _s5_weights(p):
    lam_re = jnp.minimum(p['s_lam_re'], -1e-4)
    lam_im = p['s_lam_im']
    dt = jnp.exp(p['s_log_dt'])[:, None]
    mag = jnp.exp(lam_re * dt)
    lb_re = mag * jnp.cos(lam_im * dt)
    lb_im = mag * jnp.sin(lam_im * dt)
    den = lam_re * lam_re + lam_im * lam_im
    nr = lb_re - 1.0
    coef_re = (nr * lam_re + lb_im * lam_im) / den
    coef_im = (lb_im * lam_re - nr * lam_im) / den
    b_re, b_im = p['s_b_re'], p['s_b_im']
    bb_re = coef_re[..., None] * b_re - coef_im[..., None] * b_im
    bb_im = coef_re[..., None] * b_im + coef_im[..., None] * b_re
    nblk = S_GROUPS * S_STATE // S5_COLS
    gpb = S_GROUPS // nblk
    eye = jnp.eye(gpb, dtype=F32)

    def pack_b(bb):
        x = bb.reshape(nblk, gpb, S_STATE, S_GROUP)
        return jnp.einsum('ngpc,gh->ngchp', x, eye).reshape(nblk, gpb * S_GROUP, gpb * S_STATE).astype(BF16)

    def pack_c(cc):
        x = cc.reshape(nblk, gpb, S_GROUP, S_STATE)
        return jnp.einsum('ngcp,gh->ngphc', x, eye).reshape(nblk, gpb * S_STATE, gpb * S_GROUP).astype(BF16)

    return (pack_b(bb_re), pack_b(bb_im), pack_c(p['s_c_re']), pack_c(p['s_c_im']),
            lb_re.reshape(1, -1), lb_im.reshape(1, -1))


def _s5(z_main, s0re, s0im, p, bsz, t):
    n = z_main.shape[0]
    tc = min(t, 128)
    rows = tc * bsz
    nblk = S_GROUPS * S_STATE // S5_COLS
    bre, bim, cre, cim, lre, lim = _s5_weights(p)
    u_col0 = 4 * D_MODEL // S5_CH
    blk = lambda j, c: (j, 0, 0)
    cols = lambda j, c: (0, j)
    return pl.pallas_call(
        functools.partial(_s5_kernel, bsz=bsz),
        grid=(nblk, t // tc),
        in_specs=[pl.BlockSpec((rows, S5_CH), lambda j, c: (c, u_col0 + j)),
                  pl.BlockSpec((1, S5_CH, S5_COLS), blk),
                  pl.BlockSpec((1, S5_CH, S5_COLS), blk),
                  pl.BlockSpec((1, S5_COLS, S5_CH), blk),
                  pl.BlockSpec((1, S5_COLS, S5_CH), blk),
                  pl.BlockSpec((1, S5_COLS), cols),
                  pl.BlockSpec((1, S5_COLS), cols),
                  pl.BlockSpec((1, S5_CH), cols),
                  pl.BlockSpec((bsz, S5_COLS), cols),
                  pl.BlockSpec((bsz, S5_COLS), cols)],
        out_specs=(pl.BlockSpec((rows, S5_CH), lambda j, c: (c, j)),
                   pl.BlockSpec((bsz, S5_COLS), cols),
                   pl.BlockSpec((bsz, S5_COLS), cols)),
        out_shape=(jax.ShapeDtypeStruct((n, D_MODEL), F32),
                   jax.ShapeDtypeStruct((bsz, S_GROUPS * S_STATE), F32),
                   jax.ShapeDtypeStruct((bsz, S_GROUPS * S_STATE), F32)),
        scratch_shapes=[pltpu.VMEM((rows, S5_COLS), F32), pltpu.VMEM((rows, S5_COLS), F32),
                        pltpu.VMEM((bsz, S5_COLS), F32), pltpu.VMEM((bsz, S5_COLS), F32)],
        compiler_params=_cparams(("arbitrary", "arbitrary")),
        name="s5_scan",
    )(z_main, bre, bim, cre, cim, lre, lim, p['s_d'].reshape(1, -1), s0re, s0im)


def _merge_kernel(x_ref, am_ref, yr_ref, ys_ref, zr_ref, zs_ref, g0_ref, g1_ref, g2_ref,
                  wbm_ref, wbr_ref, wbs_ref, wglu_ref, bglu_ref, wout_ref, lg_ref, lb_ref, o_ref):
    y_m = _dot(am_ref[...], wbm_ref[...])
    y_r = _dot(yr_ref[...] * _silu(zr_ref[...]), wbr_ref[...])
    ys = ys_ref[...]
    gl = ys * _sigmoid(_dot(ys, wglu_ref[...]) + bglu_ref[...])
    y_s = _dot(gl * _silu(zs_ref[...]), wbs_ref[...])
    merged = _sigmoid(g0_ref[...]) * y_m + _sigmoid(g1_ref[...]) * y_r + _sigmoid(g2_ref[...]) * y_s
    h = ALPHA * x_ref[...] + _dot(merged, wout_ref[...])
    mu = jnp.mean(h, -1, keepdims=True)
    hc = h - mu
    var = jnp.mean(hc * hc, -1, keepdims=True)
    o_ref[...] = hc * lax.rsqrt(var + LN_EPS) * lg_ref[...] + lb_ref[...]


def _merge(x, am, yr, ys, z_main, p):
    n = x.shape[0]
    tm = min(n, 512)
    row = lambda i: (i, 0)
    zcol = lambda c: (lambda i: (i, c))
    fix = lambda i: (0, 0)
    act = pl.BlockSpec((tm, D_MODEL), row)
    wsp = pl.BlockSpec((D_MODEL, D_MODEL), fix, pipeline_mode=pl.Buffered(1))
    vsp = pl.BlockSpec((1, D_MODEL), fix)
    return pl.pallas_call(
        _merge_kernel,
        grid=(n // tm,),
        in_specs=[act, act, act, act,
                  pl.BlockSpec((tm, D_MODEL), zcol(3)), pl.BlockSpec((tm, D_MODEL), zcol(5)),
                  pl.BlockSpec((tm, D_MODEL), zcol(6)), pl.BlockSpec((tm, D_MODEL), zcol(7)),
                  pl.BlockSpec((tm, D_MODEL), zcol(8)),
                  wsp, wsp, wsp, wsp, vsp, wsp, vsp, vsp],
        out_specs=act,
        out_shape=jax.ShapeDtypeStruct((n, D_MODEL), F32),
        compiler_params=_cparams(("arbitrary",)),
        name="merge_out",
    )(x, am, yr, ys, z_main, z_main, z_main, z_main, z_main,
      p['w_bm'].astype(BF16), p['w_br'].astype(BF16), p['w_bs'].astype(BF16),
      p['s_glu_w'].astype(BF16), p['s_glu_b'].reshape(1, -1), p['w_out'].astype(BF16),
      p['ln_g'].reshape(1, -1), p['ln_b'].reshape(1, -1))


GATE_COLS = 2 * M_HEADS


def _wperm_kernel(a_ref, b_ref, o_ref, *, shifted):
    a = a_ref[0]
    both = jnp.concatenate([a, b_ref[0]], axis=1)
    sh = pltpu.roll(both, 2 * LANES - GATE_COLS, axis=1)[:, :LANES]
    o_ref[...] = jnp.where(shifted(pl.program_id(0)), sh, a).astype(BF16)


def _permute_w(w_all, layer, nblk, src, shifted, name):
    spec = lambda off: pl.BlockSpec((1, D_MODEL, LANES), lambda k: (layer, 0, src(k) + off))
    return pl.pallas_call(
        functools.partial(_wperm_kernel, shifted=shifted),
        grid=(nblk,),
        in_specs=[spec(0), spec(1)],
        out_specs=pl.BlockSpec((D_MODEL, LANES), lambda k: (0, k)),
        out_shape=jax.ShapeDtypeStruct((D_MODEL, nblk * LANES), BF16),
        compiler_params=_cparams(("arbitrary",)),
        name=name,
    )(w_all, w_all)


def _split_w_in(w_all, layer):
    first_r = (3 * D_MODEL) // LANES
    n_main = Z_MAIN_W // LANES
    n_r = R_SHIFT_W // LANES
    w_main = _permute_w(w_all, layer, n_main,
                        lambda k: jnp.where(k >= first_r, k + n_r, k),
                        lambda k: k >= D_MODEL // LANES, "wperm_main")
    w_r = _permute_w(w_all, layer, n_r + 1,
                     lambda k: jnp.where(k == n_r, D_MODEL // LANES, k + first_r),
                     lambda k: k < n_r, "wperm_shift")
    return w_main, w_r


def _layer(x, state, p, w_main, w_r, bsz, t):
    n = t * bsz
    z_main = _matmul(x, w_main, 1536, "proj_main")
    z_r = _matmul(x, w_r, Z_R_W // 2, "proj_shift")
    gates = z_r[:, R_SHIFT_W:R_SHIFT_W + 2 * M_HEADS]

    if state is None:
        chunk = min(M_CHUNK, t)
        g_rows = gates.reshape(t, bsz, 2, M_HEADS).transpose(3, 2, 1, 0)
        am, c_new, n_new, m_new, conv_new = _mlstm_prompt(z_main, g_rows, p, bsz, t)
        m_new = m_new[:, :, 0].T
        conv_new = conv_new.reshape(CONV_W - 1, bsz, D_MODEL).transpose(1, 0, 2)
    else:
        (c_all, layer, c_acc), n0, m0, conv0 = state[:4]
        bias = jnp.stack([p['m_ig_b'], p['m_fg_b']], 0)
        gd = gates.reshape(n, 2, M_HEADS) + bias[None]
        gd = jnp.concatenate([gd, m0[:, None, :], jnp.zeros((n, 5, M_HEADS), F32)], axis=1).transpose(2, 0, 1)
        am, c_new, n_new, m_new = _mlstm_step(z_main, gd, conv0, c_all, layer, c_acc, n0.reshape(n, D_MODEL), p)
        m_new = m_new[:, :, 0].T
        conv_new = jnp.concatenate([conv0[:, 1:], z_main[:, None, :D_MODEL]], axis=1)
    n_new = n_new.reshape(bsz, M_HEADS, M_HD)

    if state is None:
        yr, wkv_new = _rwkv_prompt(z_r, p, bsz, t)
    else:
        r, kr, vr, dec, a = _rwkv_prep(z_r, state[5], p, bsz)
        nl = bsz * R_HEADS
        s0 = state[4].transpose(3, 2, 0, 1).reshape(R_HD, R_HD, nl)
        pk = tuple(_param_lanes(p[k], bsz) for k in ('r_k_k', 'r_k_a', 'r_r_k', 'r_ln_g', 'r_ln_b'))
        yr_l, s_new = _rwkv_scan(*(_to_lanes(v, t, bsz) for v in (r, dec, kr, vr, a)), pk, s0)
        yr = _from_lanes(yr_l, t, bsz)
        wkv_new = s_new.reshape(R_HD, R_HD, bsz, R_HEADS).transpose(2, 3, 1, 0)
    shift_new = z_r[n - bsz:, :R_SHIFT_W]

    if state is None:
        s0re = jnp.zeros((bsz, S_GROUPS * S_STATE), F32)
        s0im = s0re
    else:
        s0re = state[6].reshape(bsz, -1)
        s0im = state[7].reshape(bsz, -1)
    ys, sre_new, sim_new = _s5(z_main, s0re, s0im, p, bsz, t)

    x_new = _merge(x, am, yr, ys, z_main, p)
    new_state = (c_new, n_new, m_new, conv_new, wkv_new, shift_new,
                 sre_new.reshape(bsz, S_GROUPS, S_STATE), sim_new.reshape(bsz, S_GROUPS, S_STATE))
    return x_new, new_state


_PARAM_NAMES = ('w_in', 'm_conv_w', 'm_conv_b', 'm_wq', 'm_wk', 'm_wv', 'm_ig_b', 'm_fg_b', 'm_norm_g', 'm_skip',
                'r_mu', 'r_w0', 'r_w2', 'r_a0', 'r_a2', 'r_k_k', 'r_k_a', 'r_r_k', 'r_ln_g', 'r_ln_b',
                's_lam_re', 's_lam_im', 's_log_dt', 's_b_re', 's_b_im', 's_c_re', 's_c_im', 's_d',
                's_glu_w', 's_glu_b', 'w_bm', 'w_br', 'w_bs', 'w_out', 'ln_g', 'ln_b')


def kernel(x_prompt, x_sample, state_mlstm_c, state_mlstm_n, state_mlstm_m, state_mlstm_conv, state_rwkv_wkv, state_rwkv_shift, state_s5_re, state_s5_im, ln_in_g, ln_in_b, w_in, m_conv_w, m_conv_b, m_wq, m_wk, m_wv, m_ig_b, m_fg_b, m_norm_g, m_skip, r_mu, r_w0, r_w2, r_a0, r_a2, r_k_k, r_k_a, r_r_k, r_ln_g, r_ln_b, s_lam_re, s_lam_im, s_log_dt, s_b_re, s_b_im, s_c_re, s_c_im, s_d, s_glu_w, s_glu_b, w_bm, w_br, w_bs, w_out, ln_g, ln_b):
    weights = dict(zip(_PARAM_NAMES, (w_in, m_conv_w, m_conv_b, m_wq, m_wk, m_wv, m_ig_b, m_fg_b, m_norm_g, m_skip,
                                      r_mu, r_w0, r_w2, r_a0, r_a2, r_k_k, r_k_a, r_r_k, r_ln_g, r_ln_b,
                                      s_lam_re, s_lam_im, s_log_dt, s_b_re, s_b_im, s_c_re, s_c_im, s_d,
                                      s_glu_w, s_glu_b, w_bm, w_br, w_bs, w_out, ln_g, ln_b)))
    caches = (state_mlstm_c, state_mlstm_n, state_mlstm_m, state_mlstm_conv,
              state_rwkv_wkv, state_rwkv_shift, state_s5_re, state_s5_im)
    bp, tp, _ = x_prompt.shape
    bs, ts, _ = x_sample.shape
    xp = _layer_norm_rows(x_prompt.transpose(1, 0, 2).reshape(tp * bp, D_MODEL), ln_in_g, ln_in_b)
    xs = _layer_norm_rows(x_sample.transpose(1, 0, 2).reshape(ts * bs, D_MODEL), ln_in_g, ln_in_b)
    new_p = [[] for _ in caches]
    new_s = [[] for _ in caches]
    c_acc = None
    for l in range(DEPTH):
        p = {k: v[l] for k, v in weights.items() if k != 'w_in'}
        w_main, w_r = _split_w_in(w_in, l)
        xp, sp = _layer(xp, None, p, w_main, w_r, bp, tp)
        st = ((state_mlstm_c, l, c_acc),) + tuple(c[l] for c in caches[1:])
        xs, ss = _layer(xs, st, p, w_main, w_r, bs, ts)
        c_acc = ss[0]
        for i in range(len(caches)):
            new_p[i].append(sp[i])
            new_s[i].append(ss[i])
    pc = [jnp.stack(a) for a in new_p]
    sc = [c_acc] + [jnp.stack(a) for a in new_s[1:]]
    yp = xp.reshape(tp, bp, D_MODEL).transpose(1, 0, 2)
    ys = xs.reshape(ts, bs, D_MODEL).transpose(1, 0, 2)
    return (yp, ys, pc[0], sc[0], pc[1], sc[1], pc[2], sc[2], pc[3], sc[3],
            pc[4], sc[4], pc[5], sc[5], pc[6], sc[6], pc[7], sc[7])
```
